```python
import math
import jax, jax.numpy as jnp
from jax import lax
import numpy as np

D_MODEL = 2048
BATCH = 4
SEQ = 4096
DEPTH = 2

HEAD_DIM = 128
N_ATTN_HEADS = D_MODEL // (2 * HEAD_DIM)
N_CONV_GROUPS = D_MODEL // (2 * HEAD_DIM)
D_ATTN = N_ATTN_HEADS * HEAD_DIM
D_CONV = N_CONV_GROUPS * HEAD_DIM
D_MIX = D_ATTN + D_CONV
D_IN = 3 * D_ATTN + 3 * D_CONV
SPLIT_POINTS = (D_ATTN, 2 * D_ATTN, 3 * D_ATTN, 3 * D_ATTN + D_CONV, 3 * D_ATTN + 2 * D_CONV)
D_FF = 4 * D_MODEL
CONV_WIDTH = 3
DILATED_CONFIGS = ((128, 1), (512, 4), (2048, 16))
BAND_BLOCK = 128
ROPE_THETA = 10000.0
DEEPNORM_ALPHA = (2 * DEPTH) ** 0.25
DEEPNORM_BETA = (8 * DEPTH) ** -0.25
MOD_SCALE = 0.1
LN_EPS = 1e-5
RMS_EPS = 1e-6
NEG_INF = -1e30

kernel_name = "hymba_dilated_attn_shortconv_deepnorm_adaln"


def _layer_norm(x, g, b):
    xf = x.astype(jnp.float32)
    mu = jnp.mean(xf, axis=-1, keepdims=True)
    var = jnp.mean(jnp.square(xf - mu), axis=-1, keepdims=True)
    y = (xf - mu) * lax.rsqrt(var + LN_EPS) * g.astype(jnp.float32) + b.astype(jnp.float32)
    return y.astype(x.dtype)


def _group_rms_norm(x, g):
    b, s, d = x.shape
    xf = x.astype(jnp.float32).reshape(b, s, d // HEAD_DIM, HEAD_DIM)
    xf = xf * lax.rsqrt(jnp.mean(jnp.square(xf), axis=-1, keepdims=True) + RMS_EPS)
    return (xf.reshape(b, s, d) * g.astype(jnp.float32)).astype(x.dtype)


def _rotary(t):
    s, dh = t.shape[2], t.shape[3]
    inv_freq = ROPE_THETA ** (-jnp.arange(0, dh, 2, dtype=jnp.float32) / dh)
    ang = jnp.arange(s, dtype=jnp.float32)[:, None] * inv_freq[None, :]
    cos = jnp.concatenate([jnp.cos(ang), jnp.cos(ang)], axis=-1)
    sin = jnp.concatenate([jnp.sin(ang), jnp.sin(ang)], axis=-1)
    tf = t.astype(jnp.float32)
    t1, t2 = jnp.split(tf, 2, axis=-1)
    rot = jnp.concatenate([-t2, t1], axis=-1)
    return (tf * cos + rot * sin).astype(t.dtype)


def _dilated_branch(q, k, v, window, dilation):
    b, h, s, dh = q.shape
    span = window // dilation
    seg = s // dilation
    nblk = -(-seg // BAND_BLOCK)
    pad = nblk * BAND_BLOCK - seg

    def to_blocks(t):
        t = t.reshape(b, h, seg, dilation, dh).transpose(0, 1, 3, 2, 4)
        t = jnp.pad(t, ((0, 0), (0, 0), (0, 0), (0, pad), (0, 0)))
        return t.reshape(b, h, dilation, nblk, BAND_BLOCK, dh)

    def with_prev(t):
        prev = jnp.pad(t[:, :, :, :-1], ((0, 0), (0, 0), (0, 0), (1, 0), (0, 0), (0, 0)))
        return jnp.concatenate([prev, t], axis=4)

    qb = to_blocks(q)
    kk = with_prev(to_blocks(k))
    vv = with_prev(to_blocks(v))
    scores = jnp.einsum('bhrnqd,bhrnkd->bhrnqk', qb, kk,
                        preferred_element_type=jnp.float32) * (dh ** -0.5)
    qi = jnp.arange(BAND_BLOCK)[:, None]
    kj = jnp.arange(2 * BAND_BLOCK)[None, :]
    dist = BAND_BLOCK + qi - kj
    band = (dist >= 0) & (dist <= span)
    has_prev = (jnp.arange(nblk)[:, None, None] > 0) | (kj[None] >= BAND_BLOCK)
    valid = band[None] & has_prev
    scores = jnp.where(valid, scores, NEG_INF)
    lse = jax.nn.logsumexp(scores, axis=-1)
    probs = jnp.exp(scores - lse[..., None])
    out = jnp.einsum('bhrnqk,bhrnkd->bhrnqd', probs, vv.astype(jnp.float32))
    out = out.reshape(b, h, dilation, nblk * BAND_BLOCK, dh)[:, :, :, :seg]
    out = out.transpose(0, 1, 3, 2, 4).reshape(b, h, s, dh)
    lse = lse.reshape(b, h, dilation, nblk * BAND_BLOCK)[..., :seg]
    lse = lse.transpose(0, 1, 3, 2).reshape(b, h, s)
    return out, lse


def _dilated_mixture(q, k, v):
    outs, lses = [], []
    for window, dilation in DILATED_CONFIGS:
        o, l = _dilated_branch(q, k, v, window, dilation)
        outs.append(o)
        lses.append(l)
    weights = jax.nn.softmax(jnp.stack(lses, axis=0), axis=0)
    return jnp.sum(weights[..., None] * jnp.stack(outs, axis=0), axis=0)


def _causal_short_conv(h, w):
    s = h.shape[1]
    hp = jnp.pad(h, ((0, 0), (CONV_WIDTH - 1, 0), (0, 0)))
    return sum(w[i] * hp[:, i:i + s] for i in range(CONV_WIDTH))


def setup_inputs(seed: int = 0) -> dict:
    key = jax.random.key(seed)
    ks = jax.random.split(key, 16)
    f32 = jnp.float32
    col_scale = jnp.ones((D_IN,), f32).at[2 * D_ATTN:3 * D_ATTN].set(DEEPNORM_BETA)
    return {
        "x": jax.random.normal(ks[0], (BATCH, SEQ, D_MODEL), f32),
        "c": jax.random.normal(ks[1], (BATCH, D_MODEL), f32),
        "w_in": jax.random.normal(ks[2], (DEPTH, D_MODEL, D_IN), f32) * (D_MODEL ** -0.5) * col_scale,
        "conv_w": jax.random.normal(ks[3], (DEPTH, CONV_WIDTH, D_CONV), f32) * (CONV_WIDTH ** -0.5),
        "mix_norm_g": 1.0 + 0.02 * jax.random.normal(ks[4], (DEPTH, D_MIX), f32),
        "w_out": jax.random.normal(ks[5], (DEPTH, D_MIX, D_MODEL), f32) * (D_MIX ** -0.5) * DEEPNORM_BETA,
        "w_mod": jax.random.normal(ks[6], (DEPTH, D_MODEL, 6 * D_MODEL), f32) * (D_MODEL ** -0.5) * MOD_SCALE,
        "b_mod": 0.01 * jax.random.normal(ks[7], (DEPTH, 6 * D_MODEL), f32),
        "ln1_g": 1.0 + 0.02 * jax.random.normal(ks[8], (DEPTH, D_MODEL), f32),
        "ln1_b": 0.01 * jax.random.normal(ks[9], (DEPTH, D_MODEL), f32),
        "w_ff1": jax.random.normal(ks[10], (DEPTH, D_MODEL, D_FF), f32) * (D_MODEL ** -0.5) * DEEPNORM_BETA,
        "w_ff2": jax.random.normal(ks[11], (DEPTH, D_FF, D_MODEL), f32) * (D_FF ** -0.5) * DEEPNORM_BETA,
        "ln2_g": 1.0 + 0.02 * jax.random.normal(ks[12], (DEPTH, D_MODEL), f32),
        "ln2_b": 0.01 * jax.random.normal(ks[13], (DEPTH, D_MODEL), f32),
    }


def reference(x, c, w_in, conv_w, mix_norm_g, w_out, w_mod, b_mod,
              ln1_g, ln1_b, w_ff1, w_ff2, ln2_g, ln2_b):
    b, s, _ = x.shape
    cond = jax.nn.silu(c)

    def heads(t):
        return t.reshape(b, s, N_ATTN_HEADS, HEAD_DIM).transpose(0, 2, 1, 3)

    for l in range(DEPTH):
        mod = cond @ w_mod[l] + b_mod[l]
        sh1, sc1, g1, sh2, sc2, g2 = jnp.split(mod[:, None, :], 6, axis=-1)

        u = x * (1.0 + sc1) + sh1
        proj = u @ w_in[l]
        q, k, v, gate_b, gate_c, h_in = jnp.split(proj, SPLIT_POINTS, axis=-1)
        attn = _dilated_mixture(_rotary(heads(q)), _rotary(heads(k)), heads(v))
        attn = attn.transpose(0, 2, 1, 3).reshape(b, s, D_ATTN).astype(x.dtype)
        conv = gate_b * _causal_short_conv(gate_c * h_in, conv_w[l])
        mixed = _group_rms_norm(jnp.concatenate([attn, conv], axis=-1), mix_norm_g[l])
        x = _layer_norm(DEEPNORM_ALPHA * x + (1.0 + g1) * (mixed @ w_out[l]), ln1_g[l], ln1_b[l])

        u = x * (1.0 + sc2) + sh2
        ff = jnp.square(jax.nn.relu(u @ w_ff1[l])) @ w_ff2[l]
        x = _layer_norm(DEEPNORM_ALPHA * x + (1.0 + g2) * ff, ln2_g[l], ln2_b[l])
    return x
```

```python
import functools

import jax
import jax.numpy as jnp
from jax import lax
from jax.experimental import pallas as pl
from jax.experimental.pallas import tpu as pltpu

HEAD_DIM = 128
N_HEADS = 8
D_ATTN = N_HEADS * HEAD_DIM
D_CONV = N_HEADS * HEAD_DIM
CONV_WIDTH = 3
DILATIONS = (1, 4, 16)
BAND = 128
ROPE_THETA = 10000.0
LN_EPS = 1e-5
RMS_EPS = 1e-6
NEG_INF = -1e30

BF16 = jnp.bfloat16
F32 = jnp.float32

VMEM_LIMIT = 56 * 1024 * 1024


def _params(semantics):
    return pltpu.CompilerParams(dimension_semantics=semantics, vmem_limit_bytes=VMEM_LIMIT)


def _mod_kernel(c_ref, w_ref, b_ref, o_ref):
    c = c_ref[...]
    cond = c / (1.0 + jnp.exp(-c))
    o_ref[0] = jnp.dot(cond.astype(BF16), w_ref[0].astype(BF16),
                       preferred_element_type=F32) + b_ref[0]


def _modulation(c_pad, w_mod, b_mod, tn=1024):
    depth, d, n6 = w_mod.shape
    rows = c_pad.shape[0]
    return pl.pallas_call(
        _mod_kernel,
        grid=(depth, n6 // tn),
        in_specs=[pl.BlockSpec((rows, d), lambda l, n: (0, 0)),
                  pl.BlockSpec((1, d, tn), lambda l, n: (l, 0, n)),
                  pl.BlockSpec((1, 1, tn), lambda l, n: (l, 0, n))],
        out_specs=pl.BlockSpec((1, rows, tn), lambda l, n: (l, 0, n)),
        out_shape=jax.ShapeDtypeStruct((depth, rows, n6), F32),
        compiler_params=_params(("arbitrary", "arbitrary")),
        name="modulation",
    )(c_pad, w_mod, b_mod.reshape(depth, 1, n6))


def _modulate_kernel(x_ref, mod_ref, o_ref):
    mod = mod_ref[0, 0]
    o_ref[0] = (x_ref[0] * (1.0 + mod[1:2]) + mod[0:1]).astype(BF16)


def _modulate(x, mod, layer, tm=1024):
    b, s, d = x.shape
    return pl.pallas_call(
        _modulate_kernel,
        grid=(b, s // tm),
        in_specs=[pl.BlockSpec((1, tm, d), lambda i, m: (i, m, 0)),
                  pl.BlockSpec((1, 1, 6, d), lambda i, m: (layer, i, 0, 0))],
        out_specs=pl.BlockSpec((1, tm, d), lambda i, m: (i, m, 0)),
        out_shape=jax.ShapeDtypeStruct((b, s, d), BF16),
        compiler_params=_params(("arbitrary", "arbitrary")),
        name="modulate",
    )(x, mod)


def _qkv_kernel(u_ref, w_ref, cos_ref, sin_ref, o_ref):
    n = pl.program_id(0)
    acc = jnp.dot(u_ref[0], w_ref[0], preferred_element_type=F32)

    @pl.when(n < 2)
    def _():
        cos = cos_ref[...]
        sin = sin_ref[...]
        scale = jnp.where(n == 0, HEAD_DIM ** -0.5, 1.0).astype(F32)
        for h in range(N_HEADS):
            t = acc[:, h * HEAD_DIM:(h + 1) * HEAD_DIM]
            r = t * cos + pltpu.roll(t, HEAD_DIM // 2, axis=1) * sin
            o_ref[0, 0, h] = (r * scale).astype(BF16)

    @pl.when(n == 2)
    def _():
        for h in range(N_HEADS):
            o_ref[0, 0, h] = acc[:, h * HEAD_DIM:(h + 1) * HEAD_DIM].astype(BF16)


def _qkv_proj(u, w_in, layer, cos, sin, tm=1024):
    b, s, d = u.shape
    return pl.pallas_call(
        _qkv_kernel,
        grid=(3, b, s // tm),
        in_specs=[pl.BlockSpec((1, tm, d), lambda n, i, m: (i, m, 0)),
                  pl.BlockSpec((1, d, D_ATTN), lambda n, i, m: (layer, 0, n)),
                  pl.BlockSpec((tm, HEAD_DIM), lambda n, i, m: (m, 0)),
                  pl.BlockSpec((tm, HEAD_DIM), lambda n, i, m: (m, 0))],
        out_specs=pl.BlockSpec((1, 1, N_HEADS, tm, HEAD_DIM), lambda n, i, m: (n, i, 0, m, 0)),
        out_shape=jax.ShapeDtypeStruct((3, b, N_HEADS, s, HEAD_DIM), BF16),
        compiler_params=_params(("arbitrary", "arbitrary", "arbitrary")),
        name="qkv_proj",
    )(u, w_in, cos, sin)


def _store_group_rms(o_ref, rows, y, gain):
    for j in range(y.shape[1] // HEAD_DIM):
        cs = slice(j * HEAD_DIM, (j + 1) * HEAD_DIM)
        yj = y[:, cs]
        ms = jnp.mean(yj * yj, axis=-1, keepdims=True)
        o_ref[0, rows, cs] = (yj * lax.rsqrt(ms + RMS_EPS) * gain[:, cs]).astype(BF16)


def _conv_kernel(u_ref, wb_ref, wc_ref, wh_ref, cw_ref, gain_ref, o_ref, tail_ref):
    m = pl.program_id(2)
    tm = u_ref.shape[1]
    u = u_ref[0]
    gate_b = jnp.dot(u, wb_ref[0], preferred_element_type=F32)
    gate_c = jnp.dot(u, wc_ref[0], preferred_element_type=F32)
    h_in = jnp.dot(u, wh_ref[0], preferred_element_type=F32)
    g = gate_c * h_in

    @pl.when(m == 0)
    def _():
        tail_ref[...] = jnp.zeros_like(tail_ref)

    cw = cw_ref[0]
    w0, w1, w2 = cw[0:1], cw[1:2], cw[2:3]
    gain = gain_ref[0]
    y = gate_b * (w2 * g + w1 * pltpu.roll(g, 1, axis=0) + w0 * pltpu.roll(g, 2, axis=0))
    _store_group_rms(o_ref, slice(None), y, gain)

    head = g[0:8]
    ext = jnp.concatenate([tail_ref[...], head], axis=0)
    yh = gate_b[0:8] * (w2 * head + w1 * ext[7:15] + w0 * ext[6:14])
    _store_group_rms(o_ref, slice(0, 8), yh, gain)
    tail_ref[...] = g[tm - 8:tm]


def _conv_proj(u, w_in, conv_w, mix_gain, layer, tm=1024, tn=512):
    b, s, d = u.shape
    base = 3 * D_ATTN // tn
    step = D_CONV // tn

    def w_spec(j):
        return pl.BlockSpec((1, d, tn), lambda n, i, m: (layer, 0, base + j * step + n))

    return pl.pallas_call(
        _conv_kernel,
        grid=(D_CONV // tn, b, s // tm),
        in_specs=[pl.BlockSpec((1, tm, d), lambda n, i, m: (i, m, 0)),
                  w_spec(0), w_spec(1), w_spec(2),
                  pl.BlockSpec((1, CONV_WIDTH, tn), lambda n, i, m: (layer, 0, n)),
                  pl.BlockSpec((1, 1, tn), lambda n, i, m: (layer, 0, D_ATTN // tn + n))],
        out_specs=pl.BlockSpec((1, tm, tn), lambda n, i, m: (i, m, n)),
        out_shape=jax.ShapeDtypeStruct((b, s, D_CONV), BF16),
        scratch_shapes=[pltpu.VMEM((8, tn), F32)],
        compiler_params=_params(("arbitrary", "arbitrary", "arbitrary")),
        name="conv_proj",
    )(u, w_in, w_in, w_in, conv_w, mix_gain)


def _attn_block(q, kw, vw, bias):
    s = lax.dot_general(q, kw, (((1,), (1,)), ((), ())), preferred_element_type=F32) + bias
    mx = jnp.max(s, axis=-1, keepdims=True)
    p = jnp.exp(s - mx)
    l = jnp.sum(p, axis=-1, keepdims=True)
    o = jnp.dot(p.astype(BF16), vw, preferred_element_type=F32) * (1.0 / l)
    lse = mx + jnp.log(l)
    return o, jnp.broadcast_to(lse, (BAND, HEAD_DIM))


def _attn_kernel(q1, k1, v1, q4, k4, v4, q16, k16, v16, gain_ref, o_ref,
                 o1_s, l1_s, o4_s, l4_s, o16_s, l16_s):
    seq = o_ref.shape[1]
    qi = lax.broadcasted_iota(jnp.int32, (BAND, 2 * BAND), 0)
    kj = lax.broadcasted_iota(jnp.int32, (BAND, 2 * BAND), 1)
    dist = BAND + qi - kj
    bias_band = jnp.where((dist >= 0) & (dist <= BAND), 0.0, NEG_INF).astype(F32)
    bias_first = bias_band[:, BAND:]

    branches = ((1, q1, k1, v1, o1_s, l1_s), (4, q4, k4, v4, o4_s, l4_s),
                (16, q16, k16, v16, o16_s, l16_s))
    for d, q_ref, k_ref, v_ref, o_s, l_s in branches:
        nblk = seq // d // BAND
        for r in range(d):
            cs = slice(r * HEAD_DIM, (r + 1) * HEAD_DIM)

            def put(blk, o, lse, d=d, r=r, o_s=o_s, l_s=l_s):
                if d == 1:
                    rows = pl.ds(pl.multiple_of(blk * BAND, BAND), BAND)
                else:
                    rows = pl.ds(blk * (BAND * d) + r, BAND, stride=d)
                o_s[rows, :] = o
                l_s[rows, :] = lse

            o, lse = _attn_block(q_ref[0, 0, 0, 0:BAND, cs], k_ref[0, 0, 0, 0:BAND, cs],
                                 v_ref[0, 0, 0, 0:BAND, cs], bias_first)
            put(0, o, lse)

            def body(blk, carry, q_ref=q_ref, k_ref=k_ref, v_ref=v_ref, cs=cs, put=put):
                row = pl.multiple_of(blk * BAND, BAND)
                win = pl.ds(row - BAND, 2 * BAND)
                o, lse = _attn_block(q_ref[0, 0, 0, pl.ds(row, BAND), cs],
                                     k_ref[0, 0, 0, win, cs], v_ref[0, 0, 0, win, cs], bias_band)
                put(blk, o, lse)
                return carry

            lax.fori_loop(1, nblk, body, 0)

    gain = gain_ref[0]
    chunk = 256

    def combine(i, carry):
        rows = pl.ds(pl.multiple_of(i * chunk, chunk), chunk)
        la, lb, lc = l1_s[rows, :], l4_s[rows, :], l16_s[rows, :]
        top = jnp.maximum(jnp.maximum(la, lb), lc)
        ea, eb, ec = jnp.exp(la - top), jnp.exp(lb - top), jnp.exp(lc - top)
        o = (ea * o1_s[rows, :] + eb * o4_s[rows, :] + ec * o16_s[rows, :]) / (ea + eb + ec)
        ms = jnp.mean(o * o, axis=-1, keepdims=True)
        o_ref[0, rows, :] = (o * lax.rsqrt(ms + RMS_EPS) * gain).astype(BF16)
        return carry

    lax.fori_loop(0, seq // chunk, combine, 0)


def _attention(qkv, mix_gain, layer):
    _, b, nh, s, hd = qkv.shape
    views = [qkv.reshape(3, b, nh, s // d, d * hd) for d in DILATIONS]
    operands, specs = [], []
    for view, d in zip(views, DILATIONS):
        for which in range(3):
            operands.append(view)
            specs.append(pl.BlockSpec((1, 1, 1, s // d, d * hd),
                                      lambda i, h, which=which: (which, i, h, 0, 0)))
    specs.append(pl.BlockSpec((1, 1, hd), lambda i, h: (layer, 0, h)))
    return pl.pallas_call(
        _attn_kernel,
        grid=(b, nh),
        in_specs=specs,
        out_specs=pl.BlockSpec((1, s, hd), lambda i, h: (i, 0, h)),
        out_shape=jax.ShapeDtypeStruct((b, s, nh * hd), BF16),
        scratch_shapes=[pltpu.VMEM((s, hd), F32) for _ in range(6)],
        compiler_params=_params(("arbitrary", "arbitrary")),
        name="dilated_attention",
    )(*operands, mix_gain)


def _residual_layer_norm(x, y, gate, g, b, alpha):
    z = alpha * x + (1.0 + gate) * y
    mu = jnp.mean(z, axis=-1, keepdims=True)
    zc = z - mu
    var = jnp.mean(zc * zc, axis=-1, keepdims=True)
    return zc * lax.rsqrt(var + LN_EPS) * g + b


def _outproj_kernel(a_ref, c_ref, w_ref, x_ref, mod_ref, g_ref, b_ref, xo_ref, uo_ref, *, alpha):
    y = (jnp.dot(a_ref[0], w_ref[0, 0:D_ATTN, :], preferred_element_type=F32)
         + jnp.dot(c_ref[0], w_ref[0, D_ATTN:, :], preferred_element_type=F32))
    mod = mod_ref[0, 0]
    xn = _residual_layer_norm(x_ref[0], y, mod[2:3], g_ref[0], b_ref[0], alpha)
    xo_ref[0] = xn
    uo_ref[0] = (xn * (1.0 + mod[4:5]) + mod[3:4]).astype(BF16)


def _out_proj(attn, conv, w_out, x, mod, ln_g, ln_b, layer, alpha, tm=512):
    b, s, d = x.shape
    row = lambda i, m: (i, m, 0)
    return pl.pallas_call(
        functools.partial(_outproj_kernel, alpha=alpha),
        grid=(b, s // tm),
        in_specs=[pl.BlockSpec((1, tm, D_ATTN), row),
                  pl.BlockSpec((1, tm, D_CONV), row),
                  pl.BlockSpec((1, D_ATTN + D_CONV, d), lambda i, m: (layer, 0, 0)),
                  pl.BlockSpec((1, tm, d), row),
                  pl.BlockSpec((1, 1, 6, d), lambda i, m: (layer, i, 0, 0)),
                  pl.BlockSpec((1, 1, d), lambda i, m: (layer, 0, 0)),
                  pl.BlockSpec((1, 1, d), lambda i, m: (layer, 0, 0))],
        out_specs=[pl.BlockSpec((1, tm, d), row), pl.BlockSpec((1, tm, d), row)],
        out_shape=[jax.ShapeDtypeStruct((b, s, d), F32), jax.ShapeDtypeStruct((b, s, d), BF16)],
        compiler_params=_params(("arbitrary", "arbitrary")),
        name="out_proj_ln",
    )(attn, conv, w_out, x, mod, ln_g, ln_b)


def _ffn_kernel(u_ref, w1_ref, w2_ref, x_ref, mod_ref, modn_ref, g_ref, b_ref, *rest,
                alpha, emit_next):
    if emit_next:
        xo_ref, uo_ref, acc_ref = rest
    else:
        xo_ref, acc_ref = rest
    f = pl.program_id(2)
    h = jnp.dot(u_ref[0], w1_ref[0], preferred_element_type=F32)
    h = jnp.square(jnp.maximum(h, 0.0)).astype(BF16)
    part = jnp.dot(h, w2_ref[0], preferred_element_type=F32)

    @pl.when(f == 0)
    def _():
        acc_ref[...] = part

    @pl.when(f > 0)
    def _():
        acc_ref[...] += part

    @pl.when(f == pl.num_programs(2) - 1)
    def _():
        mod = mod_ref[0, 0]
        xn = _residual_layer_norm(x_ref[0], acc_ref[...], mod[5:6], g_ref[0], b_ref[0], alpha)
        xo_ref[0] = xn
        if emit_next:
            modn = modn_ref[0, 0]
            uo_ref[0] = (xn * (1.0 + modn[1:2]) + modn[0:1]).astype(BF16)


def _ffn(u, w1, w2, x, mod, ln_g, ln_b, layer, alpha, emit_next, tm=512, tf=512):
    b, s, d = x.shape
    d_ff = w1.shape[2]
    next_layer = layer + 1 if emit_next else layer
    row = lambda i, m, f: (i, m, 0)
    out_specs = [pl.BlockSpec((1, tm, d), row)]
    out_shape = [jax.ShapeDtypeStruct((b, s, d), F32)]
    if emit_next:
        out_specs.append(pl.BlockSpec((1, tm, d), row))
        out_shape.append(jax.ShapeDtypeStruct((b, s, d), BF16))
    return pl.pallas_call(
        functools.partial(_ffn_kernel, alpha=alpha, emit_next=emit_next),
        grid=(b, s // tm, d_ff // tf),
        in_specs=[pl.BlockSpec((1, tm, d), row),
                  pl.BlockSpec((1, d, tf), lambda i, m, f: (layer, 0, f)),
                  pl.BlockSpec((1, tf, d), lambda i, m, f: (layer, f, 0)),
                  pl.BlockSpec((1, tm, d), row),
                  pl.BlockSpec((1, 1, 6, d), lambda i, m, f: (layer, i, 0, 0)),
                  pl.BlockSpec((1, 1, 6, d), lambda i, m, f: (next_layer, i, 0, 0)),
                  pl.BlockSpec((1, 1, d), lambda i, m, f: (layer, 0, 0)),
                  pl.BlockSpec((1, 1, d), lambda i, m, f: (layer, 0, 0))],
        out_specs=out_specs,
        out_shape=out_shape,
        scratch_shapes=[pltpu.VMEM((tm, d), F32)],
        compiler_params=_params(("arbitrary", "arbitrary", "arbitrary")),
        name="ffn_ln",
    )(u, w1, w2, x, mod, mod, ln_g, ln_b)


def _rotary_tables(seq):
    inv_freq = ROPE_THETA ** (-jnp.arange(0, HEAD_DIM, 2, dtype=F32) / HEAD_DIM)
    ang = jnp.arange(seq, dtype=F32)[:, None] * inv_freq[None, :]
    cos = jnp.concatenate([jnp.cos(ang), jnp.cos(ang)], axis=-1)
    sin = jnp.concatenate([-jnp.sin(ang), jnp.sin(ang)], axis=-1)
    return cos, sin


def kernel(x, c, w_in, conv_w, mix_norm_g, w_out, w_mod, b_mod, ln1_g, ln1_b, w_ff1, w_ff2,
           ln2_g, ln2_b):
    b, s, d = x.shape
    depth = w_in.shape[0]
    alpha = (2 * depth) ** 0.25
    assert d == D_ATTN + D_CONV and w_in.shape[2] == 3 * D_ATTN + 3 * D_CONV
    assert s % (BAND * DILATIONS[-1]) == 0

    c_pad = jnp.pad(c, ((0, 8 - b), (0, 0)))
    mod = _modulation(c_pad, w_mod, b_mod)[:, :b].reshape(depth, b, 6, d)
    cos, sin = _rotary_tables(s)
    w_in_h, w_out_h = w_in.astype(BF16), w_out.astype(BF16)
    w_ff1_h, w_ff2_h = w_ff1.astype(BF16), w_ff2.astype(BF16)
    mix_gain = mix_norm_g.reshape(depth, 1, d)
    ln1_g, ln1_b, ln2_g, ln2_b = (t.reshape(depth, 1, d) for t in (ln1_g, ln1_b, ln2_g, ln2_b))

    u = _modulate(x, mod, 0)
    for layer in range(depth):
        qkv = _qkv_proj(u, w_in_h, layer, cos, sin)
        conv = _conv_proj(u, w_in_h, conv_w, mix_gain, layer)
        attn = _attention(qkv, mix_gain, layer)
        x, u = _out_proj(attn, conv, w_out_h, x, mod, ln1_g, ln1_b, layer, alpha)
        last = layer == depth - 1
        outs = _ffn(u, w_ff1_h, w_ff2_h, x, mod, ln2_g, ln2_b, layer, alpha, emit_next=not last)
        x = outs[0]
        if not last:
            u = outs[1]
    return x
```

```python
import functools

import jax
import jax.numpy as jnp
from jax import lax
from jax.experimental import pallas as pl
from jax.experimental.pallas import tpu as pltpu

HEAD_DIM = 128
N_HEADS = 8
D_ATTN = N_HEADS * HEAD_DIM
D_CONV = N_HEADS * HEAD_DIM
CONV_WIDTH = 3
DILATIONS = (1, 4, 16)
BAND = 128
ROPE_THETA = 10000.0
LN_EPS = 1e-5
RMS_EPS = 1e-6
NEG_INF = -1e30

BF16 = jnp.bfloat16
F32 = jnp.float32

VMEM_LIMIT = 56 * 1024 * 1024


def _params(semantics):
    return pltpu.CompilerParams(dimension_semantics=semantics, vmem_limit_bytes=VMEM_LIMIT)


def _mod_kernel(c_ref, w_ref, b_ref, o_ref):
    c = c_ref[...]
    cond = c / (1.0 + jnp.exp(-c))
    o_ref[0] = jnp.dot(cond.astype(BF16), w_ref[0].astype(BF16),
                       preferred_element_type=F32) + b_ref[0]


def _modulation(c_pad, w_mod, b_mod, tn=1024):
    depth, d, n6 = w_mod.shape
    rows = c_pad.shape[0]
    return pl.pallas_call(
        _mod_kernel,
        grid=(depth, n6 // tn),
        in_specs=[pl.BlockSpec((rows, d), lambda l, n: (0, 0)),
                  pl.BlockSpec((1, d, tn), lambda l, n: (l, 0, n)),
                  pl.BlockSpec((1, 1, tn), lambda l, n: (l, 0, n))],
        out_specs=pl.BlockSpec((1, rows, tn), lambda l, n: (l, 0, n)),
        out_shape=jax.ShapeDtypeStruct((depth, rows, n6), F32),
        compiler_params=_params(("arbitrary", "arbitrary")),
        name="modulation",
    )(c_pad, w_mod, b_mod.reshape(depth, 1, n6))


def _modulate_kernel(x_ref, mod_ref, o_ref):
    mod = mod_ref[0, 0]
    o_ref[0] = (x_ref[0] * (1.0 + mod[1:2]) + mod[0:1]).astype(BF16)


def _modulate(x, mod, layer, tm=1024):
    b, s, d = x.shape
    return pl.pallas_call(
        _modulate_kernel,
        grid=(b, s // tm),
        in_specs=[pl.BlockSpec((1, tm, d), lambda i, m: (i, m, 0)),
                  pl.BlockSpec((1, 1, 6, d), lambda i, m: (layer, i, 0, 0))],
        out_specs=pl.BlockSpec((1, tm, d), lambda i, m: (i, m, 0)),
        out_shape=jax.ShapeDtypeStruct((b, s, d), BF16),
        compiler_params=_params(("arbitrary", "arbitrary")),
        name="modulate",
    )(x, mod)


def _store_residue_views(t, h, outs, tmp, tmp4, ones):
    o1, o4, o16 = outs
    tm = t.shape[0]

    def put(o_ref, r, val):
        o_ref[h, r, :, 0:HEAD_DIM] = val.astype(BF16)
        if ones is not None:
            o_ref[h, r, :, HEAD_DIM:] = ones[:val.shape[0]]

    put(o1, 0, t)
    tmp[h] = t
    for lo in range(4):
        t4 = tmp[h, pl.ds(lo, tm // 4, stride=4), :]
        put(o4, lo, t4)
        tmp4[4 * h + lo] = t4
    for lo in range(4):
        for hi in range(4):
            put(o16, 4 * hi + lo, tmp4[4 * h + lo, pl.ds(hi, tm // 16, stride=4), :])


def _qk_kernel(u_ref, w_ref, cos_ref, sin_ref, o1, o4, o16, tmp, tmp4):
    n = pl.program_id(0)
    acc = jnp.dot(u_ref[0], w_ref[0], preferred_element_type=F32)
    cos = cos_ref[...]
    sin = sin_ref[...]
    scale = jnp.where(n == 0, HEAD_DIM ** -0.5, 1.0).astype(F32)
    outs = (o1.at[0, 0], o4.at[0, 0], o16.at[0, 0])
    for h in range(N_HEADS):
        t = acc[:, h * HEAD_DIM:(h + 1) * HEAD_DIM]
        r = (t * cos + pltpu.roll(t, HEAD_DIM // 2, axis=1) * sin) * scale
        _store_residue_views(r, h, outs, tmp, tmp4, None)


def _v_kernel(u_ref, w_ref, o1, o4, o16, tmp, tmp4):
    acc = jnp.dot(u_ref[0], w_ref[0], preferred_element_type=F32)
    ones = jnp.ones((acc.shape[0], HEAD_DIM), BF16)
    outs = (o1.at[0], o4.at[0], o16.at[0])
    for h in range(N_HEADS):
        _store_residue_views(acc[:, h * HEAD_DIM:(h + 1) * HEAD_DIM], h, outs, tmp, tmp4, ones)


def _view_scratch(tm):
    return [pltpu.VMEM((N_HEADS, tm, HEAD_DIM), F32),
            pltpu.VMEM((4 * N_HEADS, tm // 4, HEAD_DIM), F32)]


def _qk_proj(u, w_in, layer, cos, sin, tm=1024):
    b, s, d = u.shape
    return pl.pallas_call(
        _qk_kernel,
        grid=(2, b, s // tm),
        in_specs=[pl.BlockSpec((1, tm, d), lambda n, i, m: (i, m, 0)),
                  pl.BlockSpec((1, d, D_ATTN), lambda n, i, m: (layer, 0, n)),
                  pl.BlockSpec((tm, HEAD_DIM), lambda n, i, m: (m, 0)),
                  pl.BlockSpec((tm, HEAD_DIM), lambda n, i, m: (m, 0))],
        out_specs=[pl.BlockSpec((1, 1, N_HEADS, dil, tm // dil, HEAD_DIM),
                                lambda n, i, m: (n, i, 0, 0, m, 0)) for dil in DILATIONS],
        out_shape=[jax.ShapeDtypeStruct((2, b, N_HEADS, dil, s // dil, HEAD_DIM), BF16)
                   for dil in DILATIONS],
        scratch_shapes=_view_scratch(tm),
        compiler_params=_params(("arbitrary", "arbitrary", "arbitrary")),
        name="qk_proj",
    )(u, w_in, cos, sin)


def _v_proj(u, w_in, layer, tm=512):
    b, s, d = u.shape
    return pl.pallas_call(
        _v_kernel,
        grid=(b, s // tm),
        in_specs=[pl.BlockSpec((1, tm, d), lambda i, m: (i, m, 0)),
                  pl.BlockSpec((1, d, D_ATTN), lambda i, m: (layer, 0, 2))],
        out_specs=[pl.BlockSpec((1, N_HEADS, dil, tm // dil, 2 * HEAD_DIM),
                                lambda i, m: (i, 0, 0, m, 0)) for dil in DILATIONS],
        out_shape=[jax.ShapeDtypeStruct((b, N_HEADS, dil, s // dil, 2 * HEAD_DIM), BF16)
                   for dil in DILATIONS],
        scratch_shapes=_view_scratch(tm),
        compiler_params=_params(("arbitrary", "arbitrary")),
        name="v_proj",
    )(u, w_in)


def _store_group_rms(o_ref, rows, y, gain):
    for j in range(y.shape[1] // HEAD_DIM):
        cs = slice(j * HEAD_DIM, (j + 1) * HEAD_DIM)
        yj = y[:, cs]
        ms = jnp.mean(yj * yj, axis=-1, keepdims=True)
        o_ref[0, rows, cs] = (yj * lax.rsqrt(ms + RMS_EPS) * gain[:, cs]).astype(BF16)


def _conv_kernel(u_ref, wb_ref, wc_ref, wh_ref, cw_ref, gain_ref, o_ref, tail_ref):
    m = pl.program_id(2)
    tm = u_ref.shape[1]
    u = u_ref[0]
    gate_b = jnp.dot(u, wb_ref[0], preferred_element_type=F32)
    gate_c = jnp.dot(u, wc_ref[0], preferred_element_type=F32)
    h_in = jnp.dot(u, wh_ref[0], preferred_element_type=F32)
    g = gate_c * h_in

    @pl.when(m == 0)
    def _():
        tail_ref[...] = jnp.zeros_like(tail_ref)

    cw = cw_ref[0]
    w0, w1, w2 = cw[0:1], cw[1:2], cw[2:3]
    gain = gain_ref[0]
    y = gate_b * (w2 * g + w1 * pltpu.roll(g, 1, axis=0) + w0 * pltpu.roll(g, 2, axis=0))
    _store_group_rms(o_ref, slice(None), y, gain)

    head = g[0:8]
    ext = jnp.concatenate([tail_ref[...], head], axis=0)
    yh = gate_b[0:8] * (w2 * head + w1 * ext[7:15] + w0 * ext[6:14])
    _store_group_rms(o_ref, slice(0, 8), yh, gain)
    tail_ref[...] = g[tm - 8:tm]


def _conv_proj(u, w_in, conv_w, mix_gain, layer, tm=1024, tn=512):
    b, s, d = u.shape
    base = 3 * D_ATTN // tn
    step = D_CONV // tn

    def w_spec(j):
        return pl.BlockSpec((1, d, tn), lambda n, i, m: (layer, 0, base + j * step + n))

    return pl.pallas_call(
        _conv_kernel,
        grid=(D_CONV // tn, b, s // tm),
        in_specs=[pl.BlockSpec((1, tm, d), lambda n, i, m: (i, m, 0)),
                  w_spec(0), w_spec(1), w_spec(2),
                  pl.BlockSpec((1, CONV_WIDTH, tn), lambda n, i, m: (layer, 0, n)),
                  pl.BlockSpec((1, 1, tn), lambda n, i, m: (layer, 0, D_ATTN // tn + n))],
        out_specs=pl.BlockSpec((1, tm, tn), lambda n, i, m: (i, m, n)),
        out_shape=jax.ShapeDtypeStruct((b, s, D_CONV), BF16),
        scratch_shapes=[pltpu.VMEM((8, tn), F32)],
        compiler_params=_params(("arbitrary", "arbitrary", "arbitrary")),
        name="conv_proj",
    )(u, w_in, w_in, w_in, conv_w, mix_gain)


ATTN_UNROLL = 8


def _attn_block(q, kw, vw, bias):
    s = lax.dot_general(q, kw, (((1,), (1,)), ((), ())), preferred_element_type=F32) + bias
    mx = jnp.max(s, axis=-1, keepdims=True)
    p = jnp.exp(s - mx).astype(BF16)
    ov = jnp.dot(p, vw, preferred_element_type=F32)
    l = ov[:, HEAD_DIM:]
    return ov[:, :HEAD_DIM] / l, mx + jnp.log(l)


def _attn_kernel(q1, k1, v1, q4, k4, v4, q16, k16, v16, gain_ref, o_ref, bias_s, out_s, lse_s):
    seq = o_ref.shape[1]
    qi = lax.broadcasted_iota(jnp.int32, (BAND, 2 * BAND), 0)
    kj = lax.broadcasted_iota(jnp.int32, (BAND, 2 * BAND), 1)
    bias_s[0] = jnp.where(kj <= qi, 0.0, NEG_INF).astype(F32)
    bias_s[1] = jnp.where((kj >= qi) & (kj <= qi + BAND), 0.0, NEG_INF).astype(F32)

    def put(branch, d, r, blk, nrows, o, lse):
        if d == 1:
            rows = pl.ds(pl.multiple_of(blk * BAND, BAND), nrows)
        else:
            rows = pl.ds(blk * (BAND * d) + r, nrows, stride=d)
        out_s[branch, rows, :] = o
        lse_s[branch, rows, :] = lse

    branches = ((1, q1, k1, v1), (4, q4, k4, v4))
    for branch, (d, q_ref, k_ref, v_ref) in enumerate(branches):
        nblk = seq // d // BAND

        def group(gi, carry, branch=branch, d=d, nblk=nblk, q_ref=q_ref, k_ref=k_ref, v_ref=v_ref):
            for j in range(ATTN_UNROLL):
                f = gi * ATTN_UNROLL + j
                r, blk = f // nblk, f % nblk
                row = pl.multiple_of(blk * BAND, BAND)
                win = pl.ds(pl.multiple_of(jnp.maximum(blk - 1, 0) * BAND, BAND), 2 * BAND)
                o, lse = _attn_block(q_ref[0, 0, 0, r, pl.ds(row, BAND), :],
                                     k_ref[0, 0, 0, r, win, :], v_ref[0, 0, r, win, :],
                                     bias_s[jnp.minimum(blk, 1)])
                put(branch, d, r, blk, BAND, o, lse)
            return carry

        lax.fori_loop(0, d * nblk // ATTN_UNROLL, group, 0)

    d = DILATIONS[2]
    assert seq // d == 2 * BAND
    bias_pair = bias_s[...].reshape(2 * BAND, 2 * BAND)

    def group16(gi, carry):
        for j in range(ATTN_UNROLL // 2):
            r = gi * (ATTN_UNROLL // 2) + j
            o, lse = _attn_block(q16[0, 0, 0, r], k16[0, 0, 0, r], v16[0, 0, r], bias_pair)
            put(2, d, r, 0, 2 * BAND, o, lse)
        return carry

    lax.fori_loop(0, d // (ATTN_UNROLL // 2), group16, 0)

    gain = gain_ref[0]
    chunk = 256

    def combine(i, carry):
        rows = pl.ds(pl.multiple_of(i * chunk, chunk), chunk)
        la, lb, lc = lse_s[0, rows, :], lse_s[1, rows, :], lse_s[2, rows, :]
        top = jnp.maximum(jnp.maximum(la, lb), lc)
        ea, eb, ec = jnp.exp(la - top), jnp.exp(lb - top), jnp.exp(lc - top)
        o = (ea * out_s[0, rows, :] + eb * out_s[1, rows, :] + ec * out_s[2, rows, :]) / (ea + eb + ec)
        ms = jnp.mean(o * o, axis=-1, keepdims=True)
        o_ref[0, rows, :] = (o * lax.rsqrt(ms + RMS_EPS) * gain).astype(BF16)
        return carry

    lax.fori_loop(0, seq // chunk, combine, 0)


def _attention(qk_views, v_views, mix_gain, layer):
    _, b, nh, _, s, hd = qk_views[0].shape
    operands, specs = [], []
    for qk, v, d in zip(qk_views, v_views, DILATIONS):
        operands += [qk, qk, v]
        specs += [pl.BlockSpec((1, 1, 1, d, s // d, hd), lambda i, h: (0, i, h, 0, 0, 0)),
                  pl.BlockSpec((1, 1, 1, d, s // d, hd), lambda i, h: (1, i, h, 0, 0, 0)),
                  pl.BlockSpec((1, 1, d, s // d, 2 * hd), lambda i, h: (i, h, 0, 0, 0))]
    specs.append(pl.BlockSpec((1, 1, hd), lambda i, h: (layer, 0, h)))
    return pl.pallas_call(
        _attn_kernel,
        grid=(b, nh),
        in_specs=specs,
        out_specs=pl.BlockSpec((1, s, hd), lambda i, h: (i, 0, h)),
        out_shape=jax.ShapeDtypeStruct((b, s, nh * hd), BF16),
        scratch_shapes=[pltpu.VMEM((2, BAND, 2 * BAND), F32),
                        pltpu.VMEM((len(DILATIONS), s, hd), F32),
                        pltpu.VMEM((len(DILATIONS), s, hd), F32)],
        compiler_params=_params(("arbitrary", "arbitrary")),
        name="dilated_attention",
    )(*operands, mix_gain)


def _residual_layer_norm(x, y, gate, g, b, alpha):
    z = alpha * x + (1.0 + gate) * y
    mu = jnp.mean(z, axis=-1, keepdims=True)
    zc = z - mu
    var = jnp.mean(zc * zc, axis=-1, keepdims=True)
    return zc * lax.rsqrt(var + LN_EPS) * g + b


def _outproj_kernel(a_ref, c_ref, w_ref, x_ref, mod_ref, g_ref, b_ref, xo_ref, uo_ref, *, alpha):
    y = (jnp.dot(a_ref[0], w_ref[0, 0:D_ATTN, :], preferred_element_type=F32)
         + jnp.dot(c_ref[0], w_ref[0, D_ATTN:, :], preferred_element_type=F32))
    mod = mod_ref[0, 0]
    xn = _residual_layer_norm(x_ref[0], y, mod[2:3], g_ref[0], b_ref[0], alpha)
    xo_ref[0] = xn
    uo_ref[0] = (xn * (1.0 + mod[4:5]) + mod[3:4]).astype(BF16)


def _out_proj(attn, conv, w_out, x, mod, ln_g, ln_b, layer, alpha, tm=512):
    b, s, d = x.shape
    row = lambda i, m: (i, m, 0)
    return pl.pallas_call(
        functools.partial(_outproj_kernel, alpha=alpha),
        grid=(b, s // tm),
        in_specs=[pl.BlockSpec((1, tm, D_ATTN), row),
                  pl.BlockSpec((1, tm, D_CONV), row),
                  pl.BlockSpec((1, D_ATTN + D_CONV, d), lambda i, m: (layer, 0, 0)),
                  pl.BlockSpec((1, tm, d), row),
                  pl.BlockSpec((1, 1, 6, d), lambda i, m: (layer, i, 0, 0)),
                  pl.BlockSpec((1, 1, d), lambda i, m: (layer, 0, 0)),
                  pl.BlockSpec((1, 1, d), lambda i, m: (layer, 0, 0))],
        out_specs=[pl.BlockSpec((1, tm, d), row), pl.BlockSpec((1, tm, d), row)],
        out_shape=[jax.ShapeDtypeStruct((b, s, d), F32), jax.ShapeDtypeStruct((b, s, d), BF16)],
        compiler_params=_params(("arbitrary", "arbitrary")),
        name="out_proj_ln",
    )(attn, conv, w_out, x, mod, ln_g, ln_b)


def _ffn_kernel(u_ref, w1_ref, w2_ref, x_ref, mod_ref, modn_ref, g_ref, b_ref, *rest,
                alpha, emit_next):
    if emit_next:
        xo_ref, uo_ref, acc_ref = rest
    else:
        xo_ref, acc_ref = rest
    f = pl.program_id(2)
    h = jnp.dot(u_ref[0], w1_ref[0], preferred_element_type=F32)
    h = jnp.square(jnp.maximum(h, 0.0)).astype(BF16)
    part = jnp.dot(h, w2_ref[0], preferred_element_type=F32)

    @pl.when(f == 0)
    def _():
        acc_ref[...] = part

    @pl.when(f > 0)
    def _():
        acc_ref[...] += part

    @pl.when(f == pl.num_programs(2) - 1)
    def _():
        mod = mod_ref[0, 0]
        xn = _residual_layer_norm(x_ref[0], acc_ref[...], mod[5:6], g_ref[0], b_ref[0], alpha)
        xo_ref[0] = xn
        if emit_next:
            modn = modn_ref[0, 0]
            uo_ref[0] = (xn * (1.0 + modn[1:2]) + modn[0:1]).astype(BF16)


def _ffn(u, w1, w2, x, mod, ln_g, ln_b, layer, alpha, emit_next, tm=512, tf=512):
    b, s, d = x.shape
    d_ff = w1.shape[2]
    next_layer = layer + 1 if emit_next else layer
    row = lambda i, m, f: (i, m, 0)
    out_specs = [pl.BlockSpec((1, tm, d), row)]
    out_shape = [jax.ShapeDtypeStruct((b, s, d), F32)]
    if emit_next:
        out_specs.append(pl.BlockSpec((1, tm, d), row))
        out_shape.append(jax.ShapeDtypeStruct((b, s, d), BF16))
    return pl.pallas_call(
        functools.partial(_ffn_kernel, alpha=alpha, emit_next=emit_next),
        grid=(b, s // tm, d_ff // tf),
        in_specs=[pl.BlockSpec((1, tm, d), row),
                  pl.BlockSpec((1, d, tf), lambda i, m, f: (layer, 0, f)),
                  pl.BlockSpec((1, tf, d), lambda i, m, f: (layer, f, 0)),
                  pl.BlockSpec((1, tm, d), row),
                  pl.BlockSpec((1, 1, 6, d), lambda i, m, f: (layer, i, 0, 0)),
                  pl.BlockSpec((1, 1, 6, d), lambda i, m, f: (next_layer, i, 0, 0)),
                  pl.BlockSpec((1, 1, d), lambda i, m, f: (layer, 0, 0)),
                  pl.BlockSpec((1, 1, d), lambda i, m, f: (layer, 0, 0))],
        out_specs=out_specs,
        out_shape=out_shape,
        scratch_shapes=[pltpu.VMEM((tm, d), F32)],
        compiler_params=_params(("arbitrary", "arbitrary", "arbitrary")),
        name="ffn_ln",
    )(u, w1, w2, x, mod, mod, ln_g, ln_b)


def _rotary_tables(seq):
    inv_freq = ROPE_THETA ** (-jnp.arange(0, HEAD_DIM, 2, dtype=F32) / HEAD_DIM)
    ang = jnp.arange(seq, dtype=F32)[:, None] * inv_freq[None, :]
    cos = jnp.concatenate([jnp.cos(ang), jnp.cos(ang)], axis=-1)
    sin = jnp.concatenate([-jnp.sin(ang), jnp.sin(ang)], axis=-1)
    return cos, sin


def kernel(x, c, w_in, conv_w, mix_norm_g, w_out, w_mod, b_mod, ln1_g, ln1_b, w_ff1, w_ff2,
           ln2_g, ln2_b):
    b, s, d = x.shape
    depth = w_in.shape[0]
    alpha = (2 * depth) ** 0.25
    assert d == D_ATTN + D_CONV and w_in.shape[2] == 3 * D_ATTN + 3 * D_CONV
    assert s % (BAND * DILATIONS[-1]) == 0

    c_pad = jnp.pad(c, ((0, 8 - b), (0, 0)))
    mod = _modulation(c_pad, w_mod, b_mod)[:, :b].reshape(depth, b, 6, d)
    cos, sin = _rotary_tables(s)
    w_in_h, w_out_h = w_in.astype(BF16), w_out.astype(BF16)
    w_ff1_h, w_ff2_h = w_ff1.astype(BF16), w_ff2.astype(BF16)
    mix_gain = mix_norm_g.reshape(depth, 1, d)
    ln1_g, ln1_b, ln2_g, ln2_b = (t.reshape(depth, 1, d) for t in (ln1_g, ln1_b, ln2_g, ln2_b))

    u = _modulate(x, mod, 0)
    for layer in range(depth):
        qk_views = _qk_proj(u, w_in_h, layer, cos, sin)
        v_views = _v_proj(u, w_in_h, layer)
        conv = _conv_proj(u, w_in_h, conv_w, mix_gain, layer)
        attn = _attention(qk_views, v_views, mix_gain, layer)
        x, u = _out_proj(attn, conv, w_out_h, x, mod, ln1_g, ln1_b, layer, alpha)
        last = layer == depth - 1
        outs = _ffn(u, w_ff1_h, w_ff2_h, x, mod, ln2_g, ln2_b, layer, alpha, emit_next=not last)
        x = outs[0]
        if not last:
            u = outs[1]
    return x
```

```python
import functools

import jax
import jax.numpy as jnp
from jax import lax
from jax.experimental import pallas as pl
from jax.experimental.pallas import tpu as pltpu

HEAD_DIM = 128
N_HEADS = 8
D_ATTN = N_HEADS * HEAD_DIM
D_CONV = N_HEADS * HEAD_DIM
CONV_WIDTH = 3
DILATIONS = (1, 4, 16)
BAND = 128
ROPE_THETA = 10000.0
LN_EPS = 1e-5
RMS_EPS = 1e-6
NEG_INF = -1e30

BF16 = jnp.bfloat16
F32 = jnp.float32

VMEM_LIMIT = 56 * 1024 * 1024


def _params(semantics):
    return pltpu.CompilerParams(dimension_semantics=semantics, vmem_limit_bytes=VMEM_LIMIT)


def _mod_kernel(c_ref, w_ref, b_ref, o_ref):
    c = c_ref[...]
    cond = c / (1.0 + jnp.exp(-c))
    o_ref[0] = jnp.dot(cond.astype(BF16), w_ref[0].astype(BF16),
                       preferred_element_type=F32) + b_ref[0]


def _modulation(c_pad, w_mod, b_mod, tn=1024):
    depth, d, n6 = w_mod.shape
    rows = c_pad.shape[0]
    return pl.pallas_call(
        _mod_kernel,
        grid=(depth, n6 // tn),
        in_specs=[pl.BlockSpec((rows, d), lambda l, n: (0, 0)),
                  pl.BlockSpec((1, d, tn), lambda l, n: (l, 0, n)),
                  pl.BlockSpec((1, 1, tn), lambda l, n: (l, 0, n))],
        out_specs=pl.BlockSpec((1, rows, tn), lambda l, n: (l, 0, n)),
        out_shape=jax.ShapeDtypeStruct((depth, rows, n6), F32),
        compiler_params=_params(("arbitrary", "arbitrary")),
        name="modulation",
    )(c_pad, w_mod, b_mod.reshape(depth, 1, n6))


def _modulate_kernel(x_ref, mod_ref, o_ref):
    mod = mod_ref[0, 0]
    o_ref[0] = (x_ref[0] * (1.0 + mod[1:2]) + mod[0:1]).astype(BF16)


def _modulate(x, mod, layer, tm=1024):
    b, s, d = x.shape
    return pl.pallas_call(
        _modulate_kernel,
        grid=(b, s // tm),
        in_specs=[pl.BlockSpec((1, tm, d), lambda i, m: (i, m, 0)),
                  pl.BlockSpec((1, 1, 6, d), lambda i, m: (layer, i, 0, 0))],
        out_specs=pl.BlockSpec((1, tm, d), lambda i, m: (i, m, 0)),
        out_shape=jax.ShapeDtypeStruct((b, s, d), BF16),
        compiler_params=_params(("arbitrary", "arbitrary")),
        name="modulate",
    )(x, mod)


def _store_residue_views(t, h, outs, tmp, tmp4, ones):
    o1, o4, o16 = outs
    tm = t.shape[0]

    def put(o_ref, r, val):
        o_ref[h, r, :, 0:HEAD_DIM] = val.astype(BF16)
        if ones is not None:
            o_ref[h, r, :, HEAD_DIM:] = ones[:val.shape[0]]

    put(o1, 0, t)
    tmp[h] = t
    for lo in range(4):
        t4 = tmp[h, pl.ds(lo, tm // 4, stride=4), :]
        put(o4, lo, t4)
        tmp4[4 * h + lo] = t4
    for lo in range(4):
        for hi in range(4):
            put(o16, 4 * hi + lo, tmp4[4 * h + lo, pl.ds(hi, tm // 16, stride=4), :])


def _qk_kernel(u_ref, w_ref, cos_ref, sin_ref, o1, o4, o16, tmp, tmp4):
    n = pl.program_id(0)
    cos = cos_ref[...]
    sin = sin_ref[...]
    scale = jnp.where(n == 0, HEAD_DIM ** -0.5, 1.0).astype(F32)
    outs = (o1.at[0, 0], o4.at[0, 0], o16.at[0, 0])
    for pair in range(N_HEADS // 2):
        acc = _head_pair_dot(u_ref, w_ref, pair)
        for half in range(2):
            t = acc[:, half * HEAD_DIM:(half + 1) * HEAD_DIM]
            r = (t * cos + pltpu.roll(t, HEAD_DIM // 2, axis=1) * sin) * scale
            _store_residue_views(r, 2 * pair + half, outs, tmp, tmp4, None)


def _head_pair_dot(u_ref, w_ref, pair):
    cols = slice(2 * pair * HEAD_DIM, 2 * (pair + 1) * HEAD_DIM)
    return jnp.dot(u_ref[0], w_ref[0, :, cols], preferred_element_type=F32)


def _v_kernel(u_ref, w_ref, o1, o4, o16, tmp, tmp4):
    ones = jnp.ones((u_ref.shape[1], HEAD_DIM), BF16)
    outs = (o1.at[0], o4.at[0], o16.at[0])
    for pair in range(N_HEADS // 2):
        acc = _head_pair_dot(u_ref, w_ref, pair)
        for half in range(2):
            _store_residue_views(acc[:, half * HEAD_DIM:(half + 1) * HEAD_DIM], 2 * pair + half,
                                 outs, tmp, tmp4, ones)


def _view_scratch(tm):
    return [pltpu.VMEM((N_HEADS, tm, HEAD_DIM), F32),
            pltpu.VMEM((4 * N_HEADS, tm // 4, HEAD_DIM), F32)]


def _qk_proj(u, w_in, layer, cos, sin, tm=1024):
    b, s, d = u.shape
    return pl.pallas_call(
        _qk_kernel,
        grid=(2, b, s // tm),
        in_specs=[pl.BlockSpec((1, tm, d), lambda n, i, m: (i, m, 0)),
                  pl.BlockSpec((1, d, D_ATTN), lambda n, i, m: (layer, 0, n)),
                  pl.BlockSpec((tm, HEAD_DIM), lambda n, i, m: (m, 0)),
                  pl.BlockSpec((tm, HEAD_DIM), lambda n, i, m: (m, 0))],
        out_specs=[pl.BlockSpec((1, 1, N_HEADS, dil, tm // dil, HEAD_DIM),
                                lambda n, i, m: (n, i, 0, 0, m, 0)) for dil in DILATIONS],
        out_shape=[jax.ShapeDtypeStruct((2, b, N_HEADS, dil, s // dil, HEAD_DIM), BF16)
                   for dil in DILATIONS],
        scratch_shapes=_view_scratch(tm),
        compiler_params=_params(("arbitrary", "arbitrary", "arbitrary")),
        name="qk_proj",
    )(u, w_in, cos, sin)


def _v_proj(u, w_in, layer, tm=512):
    b, s, d = u.shape
    return pl.pallas_call(
        _v_kernel,
        grid=(b, s // tm),
        in_specs=[pl.BlockSpec((1, tm, d), lambda i, m: (i, m, 0)),
                  pl.BlockSpec((1, d, D_ATTN), lambda i, m: (layer, 0, 2))],
        out_specs=[pl.BlockSpec((1, N_HEADS, dil, tm // dil, 2 * HEAD_DIM),
                                lambda i, m: (i, 0, 0, m, 0)) for dil in DILATIONS],
        out_shape=[jax.ShapeDtypeStruct((b, N_HEADS, dil, s // dil, 2 * HEAD_DIM), BF16)
                   for dil in DILATIONS],
        scratch_shapes=_view_scratch(tm),
        compiler_params=_params(("arbitrary", "arbitrary")),
        name="v_proj",
    )(u, w_in)


def _store_group_rms(o_ref, rows, col0, y, gain):
    for j in range(y.shape[1] // HEAD_DIM):
        cs = slice(j * HEAD_DIM, (j + 1) * HEAD_DIM)
        yj = y[:, cs]
        ms = jnp.mean(yj * yj, axis=-1, keepdims=True)
        out = (yj * lax.rsqrt(ms + RMS_EPS) * gain[:, cs]).astype(BF16)
        o_ref[0, rows, col0 + j * HEAD_DIM:col0 + (j + 1) * HEAD_DIM] = out


CONV_COLS = 256


def _conv_kernel(u_ref, wb_ref, wc_ref, wh_ref, cw_ref, gain_ref, o_ref, tail_ref):
    m = pl.program_id(2)
    tm = u_ref.shape[1]
    u = u_ref[0]

    @pl.when(m == 0)
    def _():
        tail_ref[...] = jnp.zeros_like(tail_ref)

    for c0 in range(0, o_ref.shape[2], CONV_COLS):
        cols = slice(c0, c0 + CONV_COLS)
        gate_b = jnp.dot(u, wb_ref[0, :, cols], preferred_element_type=F32)
        gate_c = jnp.dot(u, wc_ref[0, :, cols], preferred_element_type=F32)
        h_in = jnp.dot(u, wh_ref[0, :, cols], preferred_element_type=F32)
        g = gate_c * h_in
        cw = cw_ref[0, :, cols]
        w0, w1, w2 = cw[0:1], cw[1:2], cw[2:3]
        gain = gain_ref[0, :, cols]
        y = gate_b * (w2 * g + w1 * pltpu.roll(g, 1, axis=0) + w0 * pltpu.roll(g, 2, axis=0))
        _store_group_rms(o_ref, slice(8, tm), c0, y[8:], gain)

        head = g[0:8]
        ext = jnp.concatenate([tail_ref[:, cols], head], axis=0)
        yh = gate_b[0:8] * (w2 * head + w1 * ext[7:15] + w0 * ext[6:14])
        _store_group_rms(o_ref, slice(0, 8), c0, yh, gain)
        tail_ref[:, cols] = g[tm - 8:tm]


def _conv_proj(u, w_in, conv_w, mix_gain, layer, tm=1024, tn=512):
    b, s, d = u.shape
    base = 3 * D_ATTN // tn
    step = D_CONV // tn

    def w_spec(j):
        return pl.BlockSpec((1, d, tn), lambda n, i, m: (layer, 0, base + j * step + n))

    return pl.pallas_call(
        _conv_kernel,
        grid=(D_CONV // tn, b, s // tm),
        in_specs=[pl.BlockSpec((1, tm, d), lambda n, i, m: (i, m, 0)),
                  w_spec(0), w_spec(1), w_spec(2),
                  pl.BlockSpec((1, CONV_WIDTH, tn), lambda n, i, m: (layer, 0, n)),
                  pl.BlockSpec((1, 1, tn), lambda n, i, m: (layer, 0, D_ATTN // tn + n))],
        out_specs=pl.BlockSpec((1, tm, tn), lambda n, i, m: (i, m, n)),
        out_shape=jax.ShapeDtypeStruct((b, s, D_CONV), BF16),
        scratch_shapes=[pltpu.VMEM((8, tn), F32)],
        compiler_params=_params(("arbitrary", "arbitrary", "arbitrary")),
        name="conv_proj",
    )(u, w_in, w_in, w_in, conv_w, mix_gain)


ATTN_UNROLL = 16


def _attn_block(q, kw, vw, bias):
    s = lax.dot_general(q, kw, (((1,), (1,)), ((), ())), preferred_element_type=F32) + bias
    mx = jnp.max(s, axis=-1, keepdims=True)
    p = jnp.exp(s - mx).astype(BF16)
    ov = jnp.dot(p, vw, preferred_element_type=F32)
    l = ov[:, HEAD_DIM:]
    return ov[:, :HEAD_DIM] / l, mx + jnp.log(l)


def _attn_kernel(q1, k1, v1, q4, k4, v4, q16, k16, v16, gain_ref, o_ref, bias_s, out_s, lse_s):
    seq = o_ref.shape[1]
    qi = lax.broadcasted_iota(jnp.int32, (BAND, 2 * BAND), 0)
    kj = lax.broadcasted_iota(jnp.int32, (BAND, 2 * BAND), 1)
    bias_s[0] = jnp.where(kj <= qi, 0.0, NEG_INF).astype(F32)
    bias_s[1] = jnp.where((kj >= qi) & (kj <= qi + BAND), 0.0, NEG_INF).astype(F32)

    def put(branch, d, r, blk, nrows, o, lse):
        if d == 1:
            rows = pl.ds(pl.multiple_of(blk * BAND, BAND), nrows)
        else:
            rows = pl.ds(blk * (BAND * d) + r, nrows, stride=d)
        out_s[branch, rows, :] = o
        lse_s[branch, rows, :] = lse

    branches = ((1, q1, k1, v1), (4, q4, k4, v4))
    for branch, (d, q_ref, k_ref, v_ref) in enumerate(branches):
        nblk = seq // d // BAND

        def group(gi, carry, branch=branch, d=d, nblk=nblk, q_ref=q_ref, k_ref=k_ref, v_ref=v_ref):
            for j in range(ATTN_UNROLL):
                f = gi * ATTN_UNROLL + j
                r, blk = f // nblk, f % nblk
                row = pl.multiple_of(blk * BAND, BAND)
                win = pl.ds(pl.multiple_of(jnp.maximum(blk - 1, 0) * BAND, BAND), 2 * BAND)
                o, lse = _attn_block(q_ref[0, 0, 0, r, pl.ds(row, BAND), :],
                                     k_ref[0, 0, 0, r, win, :], v_ref[0, 0, r, win, :],
                                     bias_s[jnp.minimum(blk, 1)])
                put(branch, d, r, blk, BAND, o, lse)
            return carry

        lax.fori_loop(0, d * nblk // ATTN_UNROLL, group, 0)

    d = DILATIONS[2]
    assert seq // d == 2 * BAND
    bias_pair = bias_s[...].reshape(2 * BAND, 2 * BAND)

    def group16(gi, carry):
        for j in range(ATTN_UNROLL // 2):
            r = gi * (ATTN_UNROLL // 2) + j
            o, lse = _attn_block(q16[0, 0, 0, r], k16[0, 0, 0, r], v16[0, 0, r], bias_pair)
            put(2, d, r, 0, 2 * BAND, o, lse)
        return carry

    lax.fori_loop(0, d // (ATTN_UNROLL // 2), group16, 0)

    gain = gain_ref[0]
    chunk = 256

    def combine(i, carry):
        rows = pl.ds(pl.multiple_of(i * chunk, chunk), chunk)
        la, lb, lc = lse_s[0, rows, :], lse_s[1, rows, :], lse_s[2, rows, :]
        top = jnp.maximum(jnp.maximum(la, lb), lc)
        ea, eb, ec = jnp.exp(la - top), jnp.exp(lb - top), jnp.exp(lc - top)
        num = ea * out_s[0, rows, :] + eb * out_s[1, rows, :] + ec * out_s[2, rows, :]
        den = ea + eb + ec
        ms = jnp.mean(num * num, axis=-1, keepdims=True)
        o_ref[0, rows, :] = (num * lax.rsqrt(ms + RMS_EPS * (den * den)) * gain).astype(BF16)
        return carry

    lax.fori_loop(0, seq // chunk, combine, 0)


def _attention(qk_views, v_views, mix_gain, layer):
    _, b, nh, _, s, hd = qk_views[0].shape
    operands, specs = [], []
    for qk, v, d in zip(qk_views, v_views, DILATIONS):
        operands += [qk, qk, v]
        specs += [pl.BlockSpec((1, 1, 1, d, s // d, hd), lambda i, h: (0, i, h, 0, 0, 0)),
                  pl.BlockSpec((1, 1, 1, d, s // d, hd), lambda i, h: (1, i, h, 0, 0, 0)),
                  pl.BlockSpec((1, 1, d, s // d, 2 * hd), lambda i, h: (i, h, 0, 0, 0))]
    specs.append(pl.BlockSpec((1, 1, hd), lambda i, h: (layer, 0, h)))
    return pl.pallas_call(
        _attn_kernel,
        grid=(b, nh),
        in_specs=specs,
        out_specs=pl.BlockSpec((1, s, hd), lambda i, h: (i, 0, h)),
        out_shape=jax.ShapeDtypeStruct((b, s, nh * hd), BF16),
        scratch_shapes=[pltpu.VMEM((2, BAND, 2 * BAND), F32),
                        pltpu.VMEM((len(DILATIONS), s, hd), F32),
                        pltpu.VMEM((len(DILATIONS), s, hd), F32)],
        compiler_params=_params(("arbitrary", "arbitrary")),
        name="dilated_attention",
    )(*operands, mix_gain)


EPILOGUE_ROWS = 128


def _residual_layer_norm(x, y, gate, g, b, alpha):
    z = alpha * x + (1.0 + gate) * y
    mu = jnp.mean(z, axis=-1, keepdims=True)
    zc = z - mu
    var = jnp.mean(zc * zc, axis=-1, keepdims=True)
    return zc * lax.rsqrt(var + LN_EPS) * g + b


def _outproj_kernel(a_ref, c_ref, w_ref, x_ref, mod_ref, g_ref, b_ref, xo_ref, uo_ref, *, alpha):
    mod = mod_ref[0, 0]
    for r0 in range(0, x_ref.shape[1], EPILOGUE_ROWS):
        rows = slice(r0, r0 + EPILOGUE_ROWS)
        y = (jnp.dot(a_ref[0, rows, :], w_ref[0, 0:D_ATTN, :], preferred_element_type=F32)
             + jnp.dot(c_ref[0, rows, :], w_ref[0, D_ATTN:, :], preferred_element_type=F32))
        xn = _residual_layer_norm(x_ref[0, rows, :], y, mod[2:3], g_ref[0], b_ref[0], alpha)
        xo_ref[0, rows, :] = xn
        uo_ref[0, rows, :] = (xn * (1.0 + mod[4:5]) + mod[3:4]).astype(BF16)


def _out_proj(attn, conv, w_out, x, mod, ln_g, ln_b, layer, alpha, tm=512):
    b, s, d = x.shape
    row = lambda i, m: (i, m, 0)
    return pl.pallas_call(
        functools.partial(_outproj_kernel, alpha=alpha),
        grid=(b, s // tm),
        in_specs=[pl.BlockSpec((1, tm, D_ATTN), row),
                  pl.BlockSpec((1, tm, D_CONV), row),
                  pl.BlockSpec((1, D_ATTN + D_CONV, d), lambda i, m: (layer, 0, 0)),
                  pl.BlockSpec((1, tm, d), row),
                  pl.BlockSpec((1, 1, 6, d), lambda i, m: (layer, i, 0, 0)),
                  pl.BlockSpec((1, 1, d), lambda i, m: (layer, 0, 0)),
                  pl.BlockSpec((1, 1, d), lambda i, m: (layer, 0, 0))],
        out_specs=[pl.BlockSpec((1, tm, d), row), pl.BlockSpec((1, tm, d), row)],
        out_shape=[jax.ShapeDtypeStruct((b, s, d), F32), jax.ShapeDtypeStruct((b, s, d), BF16)],
        compiler_params=_params(("arbitrary", "arbitrary")),
        name="out_proj_ln",
    )(attn, conv, w_out, x, mod, ln_g, ln_b)


def _ffn_kernel(u_ref, w1_ref, w2_ref, x_ref, mod_ref, modn_ref, g_ref, b_ref, *rest,
                alpha, emit_next):
    if emit_next:
        xo_ref, uo_ref, acc_ref, h_ref = rest
    else:
        xo_ref, acc_ref, h_ref = rest
    f = pl.program_id(2)
    last = pl.num_programs(2) - 1

    @pl.when(f == 0)
    def _():
        acc_ref[...] = jnp.zeros_like(acc_ref)

    h = jnp.dot(u_ref[0], w1_ref[0], preferred_element_type=F32)
    h_ref[...] = jnp.square(jnp.maximum(h, 0.0)).astype(BF16)

    @pl.when(f < last)
    def _():
        acc_ref[...] += jnp.dot(h_ref[...], w2_ref[0], preferred_element_type=F32)

    @pl.when(f == last)
    def _():
        mod = mod_ref[0, 0]
        for r0 in range(0, x_ref.shape[1], EPILOGUE_ROWS):
            rows = slice(r0, r0 + EPILOGUE_ROWS)
            y = acc_ref[rows, :] + jnp.dot(h_ref[rows, :], w2_ref[0], preferred_element_type=F32)
            xn = _residual_layer_norm(x_ref[0, rows, :], y, mod[5:6], g_ref[0], b_ref[0], alpha)
            xo_ref[0, rows, :] = xn
            if emit_next:
                modn = modn_ref[0, 0]
                uo_ref[0, rows, :] = (xn * (1.0 + modn[1:2]) + modn[0:1]).astype(BF16)


def _ffn(u, w1, w2, x, mod, ln_g, ln_b, layer, alpha, emit_next, tm=512, tf=1024):
    b, s, d = x.shape
    d_ff = w1.shape[2]
    next_layer = layer + 1 if emit_next else layer
    row = lambda i, m, f: (i, m, 0)
    out_specs = [pl.BlockSpec((1, tm, d), row)]
    out_shape = [jax.ShapeDtypeStruct((b, s, d), F32)]
    if emit_next:
        out_specs.append(pl.BlockSpec((1, tm, d), row))
        out_shape.append(jax.ShapeDtypeStruct((b, s, d), BF16))
    return pl.pallas_call(
        functools.partial(_ffn_kernel, alpha=alpha, emit_next=emit_next),
        grid=(b, s // tm, d_ff // tf),
        in_specs=[pl.BlockSpec((1, tm, d), row),
                  pl.BlockSpec((1, d, tf), lambda i, m, f: (layer, 0, f)),
                  pl.BlockSpec((1, tf, d), lambda i, m, f: (layer, f, 0)),
                  pl.BlockSpec((1, tm, d), row),
                  pl.BlockSpec((1, 1, 6, d), lambda i, m, f: (layer, i, 0, 0)),
                  pl.BlockSpec((1, 1, 6, d), lambda i, m, f: (next_layer, i, 0, 0)),
                  pl.BlockSpec((1, 1, d), lambda i, m, f: (layer, 0, 0)),
                  pl.BlockSpec((1, 1, d), lambda i, m, f: (layer, 0, 0))],
        out_specs=out_specs,
        out_shape=out_shape,
        scratch_shapes=[pltpu.VMEM((tm, d), F32), pltpu.VMEM((tm, tf), BF16)],
        compiler_params=_params(("arbitrary", "arbitrary", "arbitrary")),
        name="ffn_ln",
    )(u, w1, w2, x, mod, mod, ln_g, ln_b)


def _rotary_tables(seq):
    inv_freq = ROPE_THETA ** (-jnp.arange(0, HEAD_DIM, 2, dtype=F32) / HEAD_DIM)
    ang = jnp.arange(seq, dtype=F32)[:, None] * inv_freq[None, :]
    cos = jnp.concatenate([jnp.cos(ang), jnp.cos(ang)], axis=-1)
    sin = jnp.concatenate([-jnp.sin(ang), jnp.sin(ang)], axis=-1)
    return cos, sin


def kernel(x, c, w_in, conv_w, mix_norm_g, w_out, w_mod, b_mod, ln1_g, ln1_b, w_ff1, w_ff2,
           ln2_g, ln2_b):
    b, s, d = x.shape
    depth = w_in.shape[0]
    alpha = (2 * depth) ** 0.25
    assert d == D_ATTN + D_CONV and w_in.shape[2] == 3 * D_ATTN + 3 * D_CONV
    assert s % (BAND * DILATIONS[-1]) == 0

    c_pad = jnp.pad(c, ((0, 8 - b), (0, 0)))
    mod = _modulation(c_pad, w_mod, b_mod)[:, :b].reshape(depth, b, 6, d)
    cos, sin = _rotary_tables(s)
    w_in_h, w_out_h = w_in.astype(BF16), w_out.astype(BF16)
    w_ff1_h, w_ff2_h = w_ff1.astype(BF16), w_ff2.astype(BF16)
    mix_gain = mix_norm_g.reshape(depth, 1, d)
    ln1_g, ln1_b, ln2_g, ln2_b = (t.reshape(depth, 1, d) for t in (ln1_g, ln1_b, ln2_g, ln2_b))

    u = _modulate(x, mod, 0)
    for layer in range(depth):
        qk_views = _qk_proj(u, w_in_h, layer, cos, sin)
        v_views = _v_proj(u, w_in_h, layer)
        conv = _conv_proj(u, w_in_h, conv_w, mix_gain, layer)
        attn = _attention(qk_views, v_views, mix_gain, layer)
        x, u = _out_proj(attn, conv, w_out_h, x, mod, ln1_g, ln1_b, layer, alpha)
        last = layer == depth - 1
        outs = _ffn(u, w_ff1_h, w_ff2_h, x, mod, ln2_g, ln2_b, layer, alpha, emit_next=not last)
        x = outs[0]
        if not last:
            u = outs[1]
    return x
```

```python
import functools

import jax
import jax.numpy as jnp
from jax import lax
from jax.experimental import pallas as pl
from jax.experimental.pallas import tpu as pltpu

HEAD_DIM = 128
N_HEADS = 8
D_ATTN = N_HEADS * HEAD_DIM
D_CONV = N_HEADS * HEAD_DIM
CONV_WIDTH = 3
DILATIONS = (1, 4, 16)
BAND = 128
ROPE_THETA = 10000.0
LN_EPS = 1e-5
RMS_EPS = 1e-6
NEG_INF = -1e30

BF16 = jnp.bfloat16
F32 = jnp.float32

VMEM_LIMIT = 56 * 1024 * 1024


def _params(semantics):
    return pltpu.CompilerParams(dimension_semantics=semantics, vmem_limit_bytes=VMEM_LIMIT)


def _mod_kernel(c_ref, w_ref, b_ref, o_ref):
    c = c_ref[...]
    cond = c / (1.0 + jnp.exp(-c))
    o_ref[0] = jnp.dot(cond.astype(BF16), w_ref[0].astype(BF16),
                       preferred_element_type=F32) + b_ref[0]


def _modulation(c_pad, w_mod, b_mod, tn=1024):
    depth, d, n6 = w_mod.shape
    rows = c_pad.shape[0]
    return pl.pallas_call(
        _mod_kernel,
        grid=(depth, n6 // tn),
        in_specs=[pl.BlockSpec((rows, d), lambda l, n: (0, 0)),
                  pl.BlockSpec((1, d, tn), lambda l, n: (l, 0, n)),
                  pl.BlockSpec((1, 1, tn), lambda l, n: (l, 0, n))],
        out_specs=pl.BlockSpec((1, rows, tn), lambda l, n: (l, 0, n)),
        out_shape=jax.ShapeDtypeStruct((depth, rows, n6), F32),
        compiler_params=_params(("arbitrary", "arbitrary")),
        name="modulation",
    )(c_pad, w_mod, b_mod.reshape(depth, 1, n6))


def _modulate_kernel(x_ref, mod_ref, o_ref):
    mod = mod_ref[0, 0]
    o_ref[0] = (x_ref[0] * (1.0 + mod[1:2]) + mod[0:1]).astype(BF16)


def _modulate(x, mod, layer, tm=1024):
    b, s, d = x.shape
    return pl.pallas_call(
        _modulate_kernel,
        grid=(b, s // tm),
        in_specs=[pl.BlockSpec((1, tm, d), lambda i, m: (i, m, 0)),
                  pl.BlockSpec((1, 1, 6, d), lambda i, m: (layer, i, 0, 0))],
        out_specs=pl.BlockSpec((1, tm, d), lambda i, m: (i, m, 0)),
        out_shape=jax.ShapeDtypeStruct((b, s, d), BF16),
        compiler_params=_params(("arbitrary", "arbitrary")),
        name="modulate",
    )(x, mod)


def _store_residue_views(t, h, outs, tmp, tmp4, ones):
    o1, o4, o16 = outs
    tm = t.shape[0]

    def put(o_ref, r, val):
        o_ref[h, r, :, 0:HEAD_DIM] = val.astype(BF16)
        if ones is not None:
            o_ref[h, r, :, HEAD_DIM:] = ones[:val.shape[0]]

    put(o1, 0, t)
    tmp[h] = t
    for lo in range(4):
        t4 = tmp[h, pl.ds(lo, tm // 4, stride=4), :]
        put(o4, lo, t4)
        tmp4[4 * h + lo] = t4
    for lo in range(4):
        for hi in range(4):
            put(o16, 4 * hi + lo, tmp4[4 * h + lo, pl.ds(hi, tm // 16, stride=4), :])


def _qk_kernel(u_ref, w_ref, cos_ref, sin_ref, o1, o4, o16, tmp, tmp4):
    n = pl.program_id(0)
    cos = cos_ref[...]
    sin = sin_ref[...]
    scale = jnp.where(n == 0, HEAD_DIM ** -0.5, 1.0).astype(F32)
    outs = (o1.at[0, 0], o4.at[0, 0], o16.at[0, 0])
    for pair in range(N_HEADS // 2):
        acc = _head_pair_dot(u_ref, w_ref, pair)
        for half in range(2):
            t = acc[:, half * HEAD_DIM:(half + 1) * HEAD_DIM]
            r = (t * cos + pltpu.roll(t, HEAD_DIM // 2, axis=1) * sin) * scale
            _store_residue_views(r, 2 * pair + half, outs, tmp, tmp4, None)


def _head_pair_dot(u_ref, w_ref, pair):
    cols = slice(2 * pair * HEAD_DIM, 2 * (pair + 1) * HEAD_DIM)
    return jnp.dot(u_ref[0], w_ref[0, :, cols], preferred_element_type=F32)


def _v_kernel(u_ref, w_ref, o1, o4, o16, tmp, tmp4):
    ones = jnp.ones((u_ref.shape[1], HEAD_DIM), BF16)
    outs = (o1.at[0], o4.at[0], o16.at[0])
    for pair in range(N_HEADS // 2):
        acc = _head_pair_dot(u_ref, w_ref, pair)
        for half in range(2):
            _store_residue_views(acc[:, half * HEAD_DIM:(half + 1) * HEAD_DIM], 2 * pair + half,
                                 outs, tmp, tmp4, ones)


def _view_scratch(tm):
    return [pltpu.VMEM((N_HEADS, tm, HEAD_DIM), F32),
            pltpu.VMEM((4 * N_HEADS, tm // 4, HEAD_DIM), F32)]


def _qk_proj(u, w_in, layer, cos, sin, tm=1024):
    b, s, d = u.shape
    return pl.pallas_call(
        _qk_kernel,
        grid=(2, b, s // tm),
        in_specs=[pl.BlockSpec((1, tm, d), lambda n, i, m: (i, m, 0)),
                  pl.BlockSpec((1, d, D_ATTN), lambda n, i, m: (layer, 0, n)),
                  pl.BlockSpec((tm, HEAD_DIM), lambda n, i, m: (m, 0)),
                  pl.BlockSpec((tm, HEAD_DIM), lambda n, i, m: (m, 0))],
        out_specs=[pl.BlockSpec((1, 1, N_HEADS, dil, tm // dil, HEAD_DIM),
                                lambda n, i, m: (n, i, 0, 0, m, 0)) for dil in DILATIONS],
        out_shape=[jax.ShapeDtypeStruct((2, b, N_HEADS, dil, s // dil, HEAD_DIM), BF16)
                   for dil in DILATIONS],
        scratch_shapes=_view_scratch(tm),
        compiler_params=_params(("arbitrary", "arbitrary", "arbitrary")),
        name="qk_proj",
    )(u, w_in, cos, sin)


def _v_proj(u, w_in, layer, tm=1024):
    b, s, d = u.shape
    return pl.pallas_call(
        _v_kernel,
        grid=(b, s // tm),
        in_specs=[pl.BlockSpec((1, tm, d), lambda i, m: (i, m, 0)),
                  pl.BlockSpec((1, d, D_ATTN), lambda i, m: (layer, 0, 2))],
        out_specs=[pl.BlockSpec((1, N_HEADS, dil, tm // dil, 2 * HEAD_DIM),
                                lambda i, m: (i, 0, 0, m, 0)) for dil in DILATIONS],
        out_shape=[jax.ShapeDtypeStruct((b, N_HEADS, dil, s // dil, 2 * HEAD_DIM), BF16)
                   for dil in DILATIONS],
        scratch_shapes=_view_scratch(tm),
        compiler_params=_params(("arbitrary", "arbitrary")),
        name="v_proj",
    )(u, w_in)


def _store_group_rms(o_ref, rows, col0, y, gain):
    for j in range(y.shape[1] // HEAD_DIM):
        cs = slice(j * HEAD_DIM, (j + 1) * HEAD_DIM)
        yj = y[:, cs]
        ms = jnp.mean(yj * yj, axis=-1, keepdims=True)
        out = (yj * lax.rsqrt(ms + RMS_EPS) * gain[:, cs]).astype(BF16)
        o_ref[0, rows, col0 + j * HEAD_DIM:col0 + (j + 1) * HEAD_DIM] = out


CONV_COLS = 256


def _conv_kernel(u_ref, wb_ref, wc_ref, wh_ref, cw_ref, gain_ref, o_ref, tail_ref):
    m = pl.program_id(2)
    tm = u_ref.shape[1]
    u = u_ref[0]

    @pl.when(m == 0)
    def _():
        tail_ref[...] = jnp.zeros_like(tail_ref)

    for c0 in range(0, o_ref.shape[2], CONV_COLS):
        cols = slice(c0, c0 + CONV_COLS)
        gate_b = jnp.dot(u, wb_ref[0, :, cols], preferred_element_type=F32)
        gate_c = jnp.dot(u, wc_ref[0, :, cols], preferred_element_type=F32)
        h_in = jnp.dot(u, wh_ref[0, :, cols], preferred_element_type=F32)
        g = gate_c * h_in
        cw = cw_ref[0, :, cols]
        w0, w1, w2 = cw[0:1], cw[1:2], cw[2:3]
        gain = gain_ref[0, :, cols]
        y = gate_b * (w2 * g + w1 * pltpu.roll(g, 1, axis=0) + w0 * pltpu.roll(g, 2, axis=0))
        _store_group_rms(o_ref, slice(8, tm), c0, y[8:], gain)

        head = g[0:8]
        ext = jnp.concatenate([tail_ref[:, cols], head], axis=0)
        yh = gate_b[0:8] * (w2 * head + w1 * ext[7:15] + w0 * ext[6:14])
        _store_group_rms(o_ref, slice(0, 8), c0, yh, gain)
        tail_ref[:, cols] = g[tm - 8:tm]


def _conv_proj(u, w_in, conv_w, mix_gain, layer, tm=1024, tn=512):
    b, s, d = u.shape
    base = 3 * D_ATTN // tn
    step = D_CONV // tn

    def w_spec(j):
        return pl.BlockSpec((1, d, tn), lambda n, i, m: (layer, 0, base + j * step + n))

    return pl.pallas_call(
        _conv_kernel,
        grid=(D_CONV // tn, b, s // tm),
        in_specs=[pl.BlockSpec((1, tm, d), lambda n, i, m: (i, m, 0)),
                  w_spec(0), w_spec(1), w_spec(2),
                  pl.BlockSpec((1, CONV_WIDTH, tn), lambda n, i, m: (layer, 0, n)),
                  pl.BlockSpec((1, 1, tn), lambda n, i, m: (layer, 0, D_ATTN // tn + n))],
        out_specs=pl.BlockSpec((1, tm, tn), lambda n, i, m: (i, m, n)),
        out_shape=jax.ShapeDtypeStruct((b, s, D_CONV), BF16),
        scratch_shapes=[pltpu.VMEM((8, tn), F32)],
        compiler_params=_params(("arbitrary", "arbitrary", "arbitrary")),
        name="conv_proj",
    )(u, w_in, w_in, w_in, conv_w, mix_gain)


ATTN_UNROLL = 32


def _attn_block(q, kw, vw, bias):
    s = lax.dot_general(q, kw, (((1,), (1,)), ((), ())), preferred_element_type=F32) + bias
    mx = jnp.max(s, axis=-1, keepdims=True)
    p = jnp.exp(s - mx).astype(BF16)
    ov = jnp.dot(p, vw, preferred_element_type=F32)
    l = ov[:, HEAD_DIM:]
    return ov[:, :HEAD_DIM] / l, mx + jnp.log(l)


def _attn_kernel(q1, k1, v1, q4, k4, v4, q16, k16, v16, gain_ref, o_ref, bias_s, out_s, lse_s):
    seq = o_ref.shape[1]
    qi = lax.broadcasted_iota(jnp.int32, (BAND, 2 * BAND), 0)
    kj = lax.broadcasted_iota(jnp.int32, (BAND, 2 * BAND), 1)
    bias_s[0] = jnp.where(kj <= qi, 0.0, NEG_INF).astype(F32)
    bias_s[1] = jnp.where((kj >= qi) & (kj <= qi + BAND), 0.0, NEG_INF).astype(F32)

    def put(branch, d, r, blk, nrows, o, lse):
        if d == 1:
            rows = pl.ds(pl.multiple_of(blk * BAND, BAND), nrows)
        else:
            rows = pl.ds(blk * (BAND * d) + r, nrows, stride=d)
        out_s[branch, rows, :] = o
        lse_s[branch, rows, :] = lse

    branches = ((1, q1, k1, v1), (4, q4, k4, v4))
    for branch, (d, q_ref, k_ref, v_ref) in enumerate(branches):
        nblk = seq // d // BAND

        def group(gi, carry, branch=branch, d=d, nblk=nblk, q_ref=q_ref, k_ref=k_ref, v_ref=v_ref):
            for j in range(ATTN_UNROLL):
                f = gi * ATTN_UNROLL + j
                r, blk = f // nblk, f % nblk
                row = pl.multiple_of(blk * BAND, BAND)
                win = pl.ds(pl.multiple_of(jnp.maximum(blk - 1, 0) * BAND, BAND), 2 * BAND)
                o, lse = _attn_block(q_ref[0, 0, 0, r, pl.ds(row, BAND), :],
                                     k_ref[0, 0, 0, r, win, :], v_ref[0, 0, r, win, :],
                                     bias_s[jnp.minimum(blk, 1)])
                put(branch, d, r, blk, BAND, o, lse)
            return carry

        lax.fori_loop(0, d * nblk // ATTN_UNROLL, group, 0)

    d = DILATIONS[2]
    assert seq // d == 2 * BAND
    bias_pair = bias_s[...].reshape(2 * BAND, 2 * BAND)

    def group16(gi, carry):
        for j in range(ATTN_UNROLL // 2):
            r = gi * (ATTN_UNROLL // 2) + j
            o, lse = _attn_block(q16[0, 0, 0, r], k16[0, 0, 0, r], v16[0, 0, r], bias_pair)
            put(2, d, r, 0, 2 * BAND, o, lse)
        return carry

    lax.fori_loop(0, d // (ATTN_UNROLL // 2), group16, 0)

    gain = gain_ref[0]
    chunk = 256

    def combine(i, carry):
        rows = pl.ds(pl.multiple_of(i * chunk, chunk), chunk)
        la, lb, lc = lse_s[0, rows, :], lse_s[1, rows, :], lse_s[2, rows, :]
        top = jnp.maximum(jnp.maximum(la, lb), lc)
        ea, eb, ec = jnp.exp(la - top), jnp.exp(lb - top), jnp.exp(lc - top)
        num = ea * out_s[0, rows, :] + eb * out_s[1, rows, :] + ec * out_s[2, rows, :]
        den = ea + eb + ec
        ms = jnp.mean(num * num, axis=-1, keepdims=True)
        o_ref[0, rows, :] = (num * lax.rsqrt(ms + RMS_EPS * (den * den)) * gain).astype(BF16)
        return carry

    lax.fori_loop(0, seq // chunk, combine, 0, unroll=4)


def _attention(qk_views, v_views, mix_gain, layer):
    _, b, nh, _, s, hd = qk_views[0].shape
    operands, specs = [], []
    for qk, v, d in zip(qk_views, v_views, DILATIONS):
        operands += [qk, qk, v]
        specs += [pl.BlockSpec((1, 1, 1, d, s // d, hd), lambda i, h: (0, i, h, 0, 0, 0)),
                  pl.BlockSpec((1, 1, 1, d, s // d, hd), lambda i, h: (1, i, h, 0, 0, 0)),
                  pl.BlockSpec((1, 1, d, s // d, 2 * hd), lambda i, h: (i, h, 0, 0, 0))]
    specs.append(pl.BlockSpec((1, 1, hd), lambda i, h: (layer, 0, h)))
    return pl.pallas_call(
        _attn_kernel,
        grid=(b, nh),
        in_specs=specs,
        out_specs=pl.BlockSpec((1, s, hd), lambda i, h: (i, 0, h)),
        out_shape=jax.ShapeDtypeStruct((b, s, nh * hd), BF16),
        scratch_shapes=[pltpu.VMEM((2, BAND, 2 * BAND), F32),
                        pltpu.VMEM((len(DILATIONS), s, hd), F32),
                        pltpu.VMEM((len(DILATIONS), s, hd), F32)],
        compiler_params=_params(("arbitrary", "arbitrary")),
        name="dilated_attention",
    )(*operands, mix_gain)


EPILOGUE_ROWS = 128


def _residual_layer_norm(x, y, gate, g, b, alpha):
    z = alpha * x + (1.0 + gate) * y
    mu = jnp.mean(z, axis=-1, keepdims=True)
    zc = z - mu
    var = jnp.mean(zc * zc, axis=-1, keepdims=True)
    return zc * lax.rsqrt(var + LN_EPS) * g + b


def _outproj_kernel(a_ref, c_ref, w_ref, x_ref, mod_ref, g_ref, b_ref, xo_ref, uo_ref, *, alpha):
    mod = mod_ref[0, 0]
    for r0 in range(0, x_ref.shape[1], EPILOGUE_ROWS):
        rows = slice(r0, r0 + EPILOGUE_ROWS)
        y = (jnp.dot(a_ref[0, rows, :], w_ref[0, 0:D_ATTN, :], preferred_element_type=F32)
             + jnp.dot(c_ref[0, rows, :], w_ref[0, D_ATTN:, :], preferred_element_type=F32))
        xn = _residual_layer_norm(x_ref[0, rows, :], y, mod[2:3], g_ref[0], b_ref[0], alpha)
        xo_ref[0, rows, :] = xn
        uo_ref[0, rows, :] = (xn * (1.0 + mod[4:5]) + mod[3:4]).astype(BF16)


def _out_proj(attn, conv, w_out, x, mod, ln_g, ln_b, layer, alpha, tm=512):
    b, s, d = x.shape
    row = lambda i, m: (i, m, 0)
    return pl.pallas_call(
        functools.partial(_outproj_kernel, alpha=alpha),
        grid=(b, s // tm),
        in_specs=[pl.BlockSpec((1, tm, D_ATTN), row),
                  pl.BlockSpec((1, tm, D_CONV), row),
                  pl.BlockSpec((1, D_ATTN + D_CONV, d), lambda i, m: (layer, 0, 0)),
                  pl.BlockSpec((1, tm, d), row),
                  pl.BlockSpec((1, 1, 6, d), lambda i, m: (layer, i, 0, 0)),
                  pl.BlockSpec((1, 1, d), lambda i, m: (layer, 0, 0)),
                  pl.BlockSpec((1, 1, d), lambda i, m: (layer, 0, 0))],
        out_specs=[pl.BlockSpec((1, tm, d), row), pl.BlockSpec((1, tm, d), row)],
        out_shape=[jax.ShapeDtypeStruct((b, s, d), F32), jax.ShapeDtypeStruct((b, s, d), BF16)],
        compiler_params=_params(("arbitrary", "arbitrary")),
        name="out_proj_ln",
    )(attn, conv, w_out, x, mod, ln_g, ln_b)


def _ffn_kernel(u_ref, w1_ref, w2_ref, x_ref, mod_ref, modn_ref, g_ref, b_ref, *rest,
                alpha, emit_next):
    if emit_next:
        xo_ref, uo_ref, acc_ref, h_ref = rest
    else:
        xo_ref, acc_ref, h_ref = rest
    f = pl.program_id(2)
    last = pl.num_programs(2) - 1

    @pl.when(f == 0)
    def _():
        acc_ref[...] = jnp.zeros_like(acc_ref)

    h = jnp.dot(u_ref[0], w1_ref[0], preferred_element_type=F32)
    h_ref[...] = jnp.square(jnp.maximum(h, 0.0)).astype(BF16)

    @pl.when(f < last)
    def _():
        acc_ref[...] += jnp.dot(h_ref[...], w2_ref[0], preferred_element_type=F32)

    @pl.when(f == last)
    def _():
        mod = mod_ref[0, 0]
        for r0 in range(0, x_ref.shape[1], EPILOGUE_ROWS):
            rows = slice(r0, r0 + EPILOGUE_ROWS)
            y = acc_ref[rows, :] + jnp.dot(h_ref[rows, :], w2_ref[0], preferred_element_type=F32)
            xn = _residual_layer_norm(x_ref[0, rows, :], y, mod[5:6], g_ref[0], b_ref[0], alpha)
            xo_ref[0, rows, :] = xn
            if emit_next:
                modn = modn_ref[0, 0]
                uo_ref[0, rows, :] = (xn * (1.0 + modn[1:2]) + modn[0:1]).astype(BF16)


def _ffn(u, w1, w2, x, mod, ln_g, ln_b, layer, alpha, emit_next, tm=512, tf=1024):
    b, s, d = x.shape
    d_ff = w1.shape[2]
    next_layer = layer + 1 if emit_next else layer
    row = lambda i, m, f: (i, m, 0)
    out_specs = [pl.BlockSpec((1, tm, d), row)]
    out_shape = [jax.ShapeDtypeStruct((b, s, d), F32)]
    if emit_next:
        out_specs.append(pl.BlockSpec((1, tm, d), row))
        out_shape.append(jax.ShapeDtypeStruct((b, s, d), BF16))
    return pl.pallas_call(
        functools.partial(_ffn_kernel, alpha=alpha, emit_next=emit_next),
        grid=(b, s // tm, d_ff // tf),
        in_specs=[pl.BlockSpec((1, tm, d), row),
                  pl.BlockSpec((1, d, tf), lambda i, m, f: (layer, 0, f)),
                  pl.BlockSpec((1, tf, d), lambda i, m, f: (layer, f, 0)),
                  pl.BlockSpec((1, tm, d), row),
                  pl.BlockSpec((1, 1, 6, d), lambda i, m, f: (layer, i, 0, 0)),
                  pl.BlockSpec((1, 1, 6, d), lambda i, m, f: (next_layer, i, 0, 0)),
                  pl.BlockSpec((1, 1, d), lambda i, m, f: (layer, 0, 0)),
                  pl.BlockSpec((1, 1, d), lambda i, m, f: (layer, 0, 0))],
        out_specs=out_specs,
        out_shape=out_shape,
        scratch_shapes=[pltpu.VMEM((tm, d), F32), pltpu.VMEM((tm, tf), BF16)],
        compiler_params=_params(("arbitrary", "arbitrary", "arbitrary")),
        name="ffn_ln",
    )(u, w1, w2, x, mod, mod, ln_g, ln_b)


def _rotary_tables(seq):
    inv_freq = ROPE_THETA ** (-jnp.arange(0, HEAD_DIM, 2, dtype=F32) / HEAD_DIM)
    ang = jnp.arange(seq, dtype=F32)[:, None] * inv_freq[None, :]
    cos = jnp.concatenate([jnp.cos(ang), jnp.cos(ang)], axis=-1)
    sin = jnp.concatenate([-jnp.sin(ang), jnp.sin(ang)], axis=-1)
    return cos, sin


def kernel(x, c, w_in, conv_w, mix_norm_g, w_out, w_mod, b_mod, ln1_g, ln1_b, w_ff1, w_ff2,
           ln2_g, ln2_b):
    b, s, d = x.shape
    depth = w_in.shape[0]
    alpha = (2 * depth) ** 0.25
    assert d == D_ATTN + D_CONV and w_in.shape[2] == 3 * D_ATTN + 3 * D_CONV
    assert s % (BAND * DILATIONS[-1]) == 0

    c_pad = jnp.pad(c, ((0, 8 - b), (0, 0)))
    mod = _modulation(c_pad, w_mod, b_mod)[:, :b].reshape(depth, b, 6, d)
    cos, sin = _rotary_tables(s)
    w_in_h, w_out_h = w_in.astype(BF16), w_out.astype(BF16)
    w_ff1_h, w_ff2_h = w_ff1.astype(BF16), w_ff2.astype(BF16)
    mix_gain = mix_norm_g.reshape(depth, 1, d)
    ln1_g, ln1_b, ln2_g, ln2_b = (t.reshape(depth, 1, d) for t in (ln1_g, ln1_b, ln2_g, ln2_b))

    u = _modulate(x, mod, 0)
    for layer in range(depth):
        qk_views = _qk_proj(u, w_in_h, layer, cos, sin)
        v_views = _v_proj(u, w_in_h, layer)
        conv = _conv_proj(u, w_in_h, conv_w, mix_gain, layer)
        attn = _attention(qk_views, v_views, mix_gain, layer)
        x, u = _out_proj(attn, conv, w_out_h, x, mod, ln1_g, ln1_b, layer, alpha)
        last = layer == depth - 1
        outs = _ffn(u, w_ff1_h, w_ff2_h, x, mod, ln2_g, ln2_b, layer, alpha, emit_next=not last)
        x = outs[0]
        if not last:
            u = outs[1]
    return x
```

```python
import functools

import jax
import jax.numpy as jnp
from jax import lax
from jax.experimental import pallas as pl
from jax.experimental.pallas import tpu as pltpu

HEAD_DIM = 128
N_HEADS = 8
D_ATTN = N_HEADS * HEAD_DIM
D_CONV = N_HEADS * HEAD_DIM
CONV_WIDTH = 3
DILATIONS = (1, 4, 16)
BAND = 128
ROPE_THETA = 10000.0
LN_EPS = 1e-5
RMS_EPS = 1e-6
NEG_INF = -1e30

BF16 = jnp.bfloat16
F32 = jnp.float32

VMEM_LIMIT = 56 * 1024 * 1024


def _params(semantics):
    return pltpu.CompilerParams(dimension_semantics=semantics, vmem_limit_bytes=VMEM_LIMIT)


def _mod_kernel(c_ref, w_ref, b_ref, o_ref):
    c = c_ref[...]
    cond = c / (1.0 + jnp.exp(-c))
    o_ref[0] = jnp.dot(cond.astype(BF16), w_ref[0].astype(BF16),
                       preferred_element_type=F32) + b_ref[0]


def _modulation(c_pad, w_mod, b_mod, tn=1024):
    depth, d, n6 = w_mod.shape
    rows = c_pad.shape[0]
    return pl.pallas_call(
        _mod_kernel,
        grid=(depth, n6 // tn),
        in_specs=[pl.BlockSpec((rows, d), lambda l, n: (0, 0)),
                  pl.BlockSpec((1, d, tn), lambda l, n: (l, 0, n)),
                  pl.BlockSpec((1, 1, tn), lambda l, n: (l, 0, n))],
        out_specs=pl.BlockSpec((1, rows, tn), lambda l, n: (l, 0, n)),
        out_shape=jax.ShapeDtypeStruct((depth, rows, n6), F32),
        compiler_params=_params(("arbitrary", "arbitrary")),
        name="modulation",
    )(c_pad, w_mod, b_mod.reshape(depth, 1, n6))


def _modulate_kernel(x_ref, mod_ref, o_ref):
    mod = mod_ref[0, 0]
    o_ref[0] = (x_ref[0] * (1.0 + mod[1:2]) + mod[0:1]).astype(BF16)


def _modulate(x, mod, layer, tm=1024):
    b, s, d = x.shape
    return pl.pallas_call(
        _modulate_kernel,
        grid=(b, s // tm),
        in_specs=[pl.BlockSpec((1, tm, d), lambda i, m: (i, m, 0)),
                  pl.BlockSpec((1, 1, 6, d), lambda i, m: (layer, i, 0, 0))],
        out_specs=pl.BlockSpec((1, tm, d), lambda i, m: (i, m, 0)),
        out_shape=jax.ShapeDtypeStruct((b, s, d), BF16),
        compiler_params=_params(("arbitrary", "arbitrary")),
        name="modulate",
    )(x, mod)


def _store_residue_views(t, h, outs, tmp, tmp4, ones):
    o1, o4, o16 = outs
    tm = t.shape[0]

    def put(o_ref, r, val):
        o_ref[h, r, :, 0:HEAD_DIM] = val.astype(BF16)
        if ones is not None:
            o_ref[h, r, :, HEAD_DIM:] = ones[:val.shape[0]]

    put(o1, 0, t)
    tmp[h] = t
    for lo in range(4):
        t4 = tmp[h, pl.ds(lo, tm // 4, stride=4), :]
        put(o4, lo, t4)
        tmp4[4 * h + lo] = t4
    for lo in range(4):
        for hi in range(4):
            put(o16, 4 * hi + lo, tmp4[4 * h + lo, pl.ds(hi, tm // 16, stride=4), :])


def _cast_weight_once(first, w_ref, w_s):
    @pl.when(first)
    def _():
        w_s[...] = w_ref[0].astype(BF16)


def _qk_kernel(u_ref, w_ref, cos_ref, sin_ref, o1, o4, o16, tmp, tmp4, w_s):
    n = pl.program_id(0)
    _cast_weight_once((pl.program_id(1) == 0) & (pl.program_id(2) == 0), w_ref, w_s)
    cos = cos_ref[...]
    sin = sin_ref[...]
    scale = jnp.where(n == 0, HEAD_DIM ** -0.5, 1.0).astype(F32)
    outs = (o1.at[0, 0], o4.at[0, 0], o16.at[0, 0])
    for pair in range(N_HEADS // 2):
        acc = _head_pair_dot(u_ref, w_s, pair)
        for half in range(2):
            t = acc[:, half * HEAD_DIM:(half + 1) * HEAD_DIM]
            r = (t * cos + pltpu.roll(t, HEAD_DIM // 2, axis=1) * sin) * scale
            _store_residue_views(r, 2 * pair + half, outs, tmp, tmp4, None)


def _head_pair_dot(u_ref, w_s, pair):
    cols = slice(2 * pair * HEAD_DIM, 2 * (pair + 1) * HEAD_DIM)
    return jnp.dot(u_ref[0], w_s[:, cols], preferred_element_type=F32)


def _v_kernel(u_ref, w_ref, o1, o4, o16, tmp, tmp4, w_s):
    _cast_weight_once((pl.program_id(0) == 0) & (pl.program_id(1) == 0), w_ref, w_s)
    ones = jnp.ones((u_ref.shape[1], HEAD_DIM), BF16)
    outs = (o1.at[0], o4.at[0], o16.at[0])
    for pair in range(N_HEADS // 2):
        acc = _head_pair_dot(u_ref, w_s, pair)
        for half in range(2):
            _store_residue_views(acc[:, half * HEAD_DIM:(half + 1) * HEAD_DIM], 2 * pair + half,
                                 outs, tmp, tmp4, ones)


def _view_scratch(tm, d):
    return [pltpu.VMEM((N_HEADS, tm, HEAD_DIM), F32),
            pltpu.VMEM((4 * N_HEADS, tm // 4, HEAD_DIM), F32),
            pltpu.VMEM((d, D_ATTN), BF16)]


def _qk_proj(u, w_in, layer, cos, sin, tm=1024):
    b, s, d = u.shape
    return pl.pallas_call(
        _qk_kernel,
        grid=(2, b, s // tm),
        in_specs=[pl.BlockSpec((1, tm, d), lambda n, i, m: (i, m, 0)),
                  pl.BlockSpec((1, d, D_ATTN), lambda n, i, m: (layer, 0, n)),
                  pl.BlockSpec((tm, HEAD_DIM), lambda n, i, m: (m, 0)),
                  pl.BlockSpec((tm, HEAD_DIM), lambda n, i, m: (m, 0))],
        out_specs=[pl.BlockSpec((1, 1, N_HEADS, dil, tm // dil, HEAD_DIM),
                                lambda n, i, m: (n, i, 0, 0, m, 0)) for dil in DILATIONS],
        out_shape=[jax.ShapeDtypeStruct((2, b, N_HEADS, dil, s // dil, HEAD_DIM), BF16)
                   for dil in DILATIONS],
        scratch_shapes=_view_scratch(tm, d),
        compiler_params=_params(("arbitrary", "arbitrary", "arbitrary")),
        name="qk_proj",
    )(u, w_in, cos, sin)


def _v_proj(u, w_in, layer, tm=512):
    b, s, d = u.shape
    return pl.pallas_call(
        _v_kernel,
        grid=(b, s // tm),
        in_specs=[pl.BlockSpec((1, tm, d), lambda i, m: (i, m, 0)),
                  pl.BlockSpec((1, d, D_ATTN), lambda i, m: (layer, 0, 2))],
        out_specs=[pl.BlockSpec((1, N_HEADS, dil, tm // dil, 2 * HEAD_DIM),
                                lambda i, m: (i, 0, 0, m, 0)) for dil in DILATIONS],
        out_shape=[jax.ShapeDtypeStruct((b, N_HEADS, dil, s // dil, 2 * HEAD_DIM), BF16)
                   for dil in DILATIONS],
        scratch_shapes=_view_scratch(tm, d),
        compiler_params=_params(("arbitrary", "arbitrary")),
        name="v_proj",
    )(u, w_in)


def _store_group_rms(o_ref, rows, col0, y, gain):
    for j in range(y.shape[1] // HEAD_DIM):
        cs = slice(j * HEAD_DIM, (j + 1) * HEAD_DIM)
        yj = y[:, cs]
        ms = jnp.mean(yj * yj, axis=-1, keepdims=True)
        out = (yj * lax.rsqrt(ms + RMS_EPS) * gain[:, cs]).astype(BF16)
        o_ref[0, rows, col0 + j * HEAD_DIM:col0 + (j + 1) * HEAD_DIM] = out


CONV_COLS = 256


def _conv_kernel(u_ref, wb_ref, wc_ref, wh_ref, cw_ref, gain_ref, o_ref, tail_ref, w_s):
    m = pl.program_id(2)
    tm = u_ref.shape[1]
    u = u_ref[0]
    first = (pl.program_id(1) == 0) & (m == 0)
    for j, w_ref in enumerate((wb_ref, wc_ref, wh_ref)):
        _cast_weight_once(first, w_ref, w_s.at[j])

    @pl.when(m == 0)
    def _():
        tail_ref[...] = jnp.zeros_like(tail_ref)

    for c0 in range(0, o_ref.shape[2], CONV_COLS):
        cols = slice(c0, c0 + CONV_COLS)
        gate_b = jnp.dot(u, w_s[0, :, cols], preferred_element_type=F32)
        gate_c = jnp.dot(u, w_s[1, :, cols], preferred_element_type=F32)
        h_in = jnp.dot(u, w_s[2, :, cols], preferred_element_type=F32)
        g = gate_c * h_in
        cw = cw_ref[0, :, cols]
        w0, w1, w2 = cw[0:1], cw[1:2], cw[2:3]
        gain = gain_ref[0, :, cols]
        y = gate_b * (w2 * g + w1 * pltpu.roll(g, 1, axis=0) + w0 * pltpu.roll(g, 2, axis=0))
        _store_group_rms(o_ref, slice(8, tm), c0, y[8:], gain)

        head = g[0:8]
        ext = jnp.concatenate([tail_ref[:, cols], head], axis=0)
        yh = gate_b[0:8] * (w2 * head + w1 * ext[7:15] + w0 * ext[6:14])
        _store_group_rms(o_ref, slice(0, 8), c0, yh, gain)
        tail_ref[:, cols] = g[tm - 8:tm]


def _conv_proj(u, w_in, conv_w, mix_gain, layer, tm=1024, tn=512):
    b, s, d = u.shape
    base = 3 * D_ATTN // tn
    step = D_CONV // tn

    def w_spec(j):
        return pl.BlockSpec((1, d, tn), lambda n, i, m: (layer, 0, base + j * step + n))

    return pl.pallas_call(
        _conv_kernel,
        grid=(D_CONV // tn, b, s // tm),
        in_specs=[pl.BlockSpec((1, tm, d), lambda n, i, m: (i, m, 0)),
                  w_spec(0), w_spec(1), w_spec(2),
                  pl.BlockSpec((1, CONV_WIDTH, tn), lambda n, i, m: (layer, 0, n)),
                  pl.BlockSpec((1, 1, tn), lambda n, i, m: (layer, 0, D_ATTN // tn + n))],
        out_specs=pl.BlockSpec((1, tm, tn), lambda n, i, m: (i, m, n)),
        out_shape=jax.ShapeDtypeStruct((b, s, D_CONV), BF16),
        scratch_shapes=[pltpu.VMEM((8, tn), F32), pltpu.VMEM((3, d, tn), BF16)],
        compiler_params=_params(("arbitrary", "arbitrary", "arbitrary")),
        name="conv_proj",
    )(u, w_in, w_in, w_in, conv_w, mix_gain)


ATTN_UNROLL = 32


def _attn_block(q, kw, vw, bias):
    s = lax.dot_general(q, kw, (((1,), (1,)), ((), ())), preferred_element_type=F32) + bias
    mx = jnp.max(s, axis=-1, keepdims=True)
    p = jnp.exp(s - mx).astype(BF16)
    ov = jnp.dot(p, vw, preferred_element_type=F32)
    l = ov[:, HEAD_DIM:]
    return ov[:, :HEAD_DIM] / l, mx + jnp.log(l)


def _attn_kernel(q1, k1, v1, q4, k4, v4, q16, k16, v16, gain_ref, o_ref, bias_s, out_s, lse_s):
    seq = o_ref.shape[1]
    qi = lax.broadcasted_iota(jnp.int32, (BAND, 2 * BAND), 0)
    kj = lax.broadcasted_iota(jnp.int32, (BAND, 2 * BAND), 1)
    bias_s[0] = jnp.where(kj <= qi, 0.0, NEG_INF).astype(F32)
    bias_s[1] = jnp.where((kj >= qi) & (kj <= qi + BAND), 0.0, NEG_INF).astype(F32)

    def put(branch, d, r, blk, nrows, o, lse):
        if d == 1:
            rows = pl.ds(pl.multiple_of(blk * BAND, BAND), nrows)
        else:
            rows = pl.ds(blk * (BAND * d) + r, nrows, stride=d)
        out_s[branch, rows, :] = o
        lse_s[branch, rows, :] = lse

    branches = ((1, q1, k1, v1), (4, q4, k4, v4))
    for branch, (d, q_ref, k_ref, v_ref) in enumerate(branches):
        nblk = seq // d // BAND

        def group(gi, carry, branch=branch, d=d, nblk=nblk, q_ref=q_ref, k_ref=k_ref, v_ref=v_ref):
            for j in range(ATTN_UNROLL):
                f = gi * ATTN_UNROLL + j
                r, blk = f // nblk, f % nblk
                row = pl.multiple_of(blk * BAND, BAND)
                win = pl.ds(pl.multiple_of(jnp.maximum(blk - 1, 0) * BAND, BAND), 2 * BAND)
                o, lse = _attn_block(q_ref[0, 0, 0, r, pl.ds(row, BAND), :],
                                     k_ref[0, 0, 0, r, win, :], v_ref[0, 0, r, win, :],
                                     bias_s[jnp.minimum(blk, 1)])
                put(branch, d, r, blk, BAND, o, lse)
            return carry

        lax.fori_loop(0, d * nblk // ATTN_UNROLL, group, 0)

    d = DILATIONS[2]
    assert seq // d == 2 * BAND
    bias_pair = bias_s[...].reshape(2 * BAND, 2 * BAND)

    def group16(gi, carry):
        for j in range(ATTN_UNROLL // 2):
            r = gi * (ATTN_UNROLL // 2) + j
            o, lse = _attn_block(q16[0, 0, 0, r], k16[0, 0, 0, r], v16[0, 0, r], bias_pair)
            put(2, d, r, 0, 2 * BAND, o, lse)
        return carry

    lax.fori_loop(0, d // (ATTN_UNROLL // 2), group16, 0)

    gain = gain_ref[0]
    chunk = 256

    def combine(i, carry):
        rows = pl.ds(pl.multiple_of(i * chunk, chunk), chunk)
        la, lb, lc = lse_s[0, rows, :], lse_s[1, rows, :], lse_s[2, rows, :]
        top = jnp.maximum(jnp.maximum(la, lb), lc)
        ea, eb, ec = jnp.exp(la - top), jnp.exp(lb - top), jnp.exp(lc - top)
        num = ea * out_s[0, rows, :] + eb * out_s[1, rows, :] + ec * out_s[2, rows, :]
        den = ea + eb + ec
        ms = jnp.mean(num * num, axis=-1, keepdims=True)
        o_ref[0, rows, :] = (num * lax.rsqrt(ms + RMS_EPS * (den * den)) * gain).astype(BF16)
        return carry

    lax.fori_loop(0, seq // chunk, combine, 0, unroll=4)


def _attention(qk_views, v_views, mix_gain, layer):
    _, b, nh, _, s, hd = qk_views[0].shape
    operands, specs = [], []
    for qk, v, d in zip(qk_views, v_views, DILATIONS):
        operands += [qk, qk, v]
        specs += [pl.BlockSpec((1, 1, 1, d, s // d, hd), lambda i, h: (0, i, h, 0, 0, 0)),
                  pl.BlockSpec((1, 1, 1, d, s // d, hd), lambda i, h: (1, i, h, 0, 0, 0)),
                  pl.BlockSpec((1, 1, d, s // d, 2 * hd), lambda i, h: (i, h, 0, 0, 0))]
    specs.append(pl.BlockSpec((1, 1, hd), lambda i, h: (layer, 0, h)))
    return pl.pallas_call(
        _attn_kernel,
        grid=(b, nh),
        in_specs=specs,
        out_specs=pl.BlockSpec((1, s, hd), lambda i, h: (i, 0, h)),
        out_shape=jax.ShapeDtypeStruct((b, s, nh * hd), BF16),
        scratch_shapes=[pltpu.VMEM((2, BAND, 2 * BAND), F32),
                        pltpu.VMEM((len(DILATIONS), s, hd), F32),
                        pltpu.VMEM((len(DILATIONS), s, hd), F32)],
        compiler_params=_params(("arbitrary", "arbitrary")),
        name="dilated_attention",
    )(*operands, mix_gain)


EPILOGUE_ROWS = 128


def _residual_layer_norm(x, y, gate, g, b, alpha):
    z = alpha * x + (1.0 + gate) * y
    mu = jnp.mean(z, axis=-1, keepdims=True)
    zc = z - mu
    var = jnp.mean(zc * zc, axis=-1, keepdims=True)
    return zc * lax.rsqrt(var + LN_EPS) * g + b


def _outproj_kernel(a_ref, c_ref, w_ref, x_ref, mod_ref, g_ref, b_ref, xo_ref, uo_ref, w_s, *,
                    alpha):
    _cast_weight_once((pl.program_id(0) == 0) & (pl.program_id(1) == 0), w_ref, w_s)
    mod = mod_ref[0, 0]
    for r0 in range(0, x_ref.shape[1], EPILOGUE_ROWS):
        rows = slice(r0, r0 + EPILOGUE_ROWS)
        y = (jnp.dot(a_ref[0, rows, :], w_s[0:D_ATTN, :], preferred_element_type=F32)
             + jnp.dot(c_ref[0, rows, :], w_s[D_ATTN:, :], preferred_element_type=F32))
        xn = _residual_layer_norm(x_ref[0, rows, :], y, mod[2:3], g_ref[0], b_ref[0], alpha)
        xo_ref[0, rows, :] = xn
        uo_ref[0, rows, :] = (xn * (1.0 + mod[4:5]) + mod[3:4]).astype(BF16)


def _out_proj(attn, conv, w_out, x, mod, ln_g, ln_b, layer, alpha, tm=512):
    b, s, d = x.shape
    row = lambda i, m: (i, m, 0)
    return pl.pallas_call(
        functools.partial(_outproj_kernel, alpha=alpha),
        grid=(b, s // tm),
        in_specs=[pl.BlockSpec((1, tm, D_ATTN), row),
                  pl.BlockSpec((1, tm, D_CONV), row),
                  pl.BlockSpec((1, D_ATTN + D_CONV, d), lambda i, m: (layer, 0, 0),
                               pipeline_mode=pl.Buffered(1)),
                  pl.BlockSpec((1, tm, d), row),
                  pl.BlockSpec((1, 1, 6, d), lambda i, m: (layer, i, 0, 0)),
                  pl.BlockSpec((1, 1, d), lambda i, m: (layer, 0, 0)),
                  pl.BlockSpec((1, 1, d), lambda i, m: (layer, 0, 0))],
        out_specs=[pl.BlockSpec((1, tm, d), row), pl.BlockSpec((1, tm, d), row)],
        out_shape=[jax.ShapeDtypeStruct((b, s, d), F32), jax.ShapeDtypeStruct((b, s, d), BF16)],
        scratch_shapes=[pltpu.VMEM((D_ATTN + D_CONV, d), BF16)],
        compiler_params=_params(("arbitrary", "arbitrary")),
        name="out_proj_ln",
    )(attn, conv, w_out, x, mod, ln_g, ln_b)


def _ffn_kernel(u_ref, w1_ref, w2_ref, x_ref, mod_ref, modn_ref, g_ref, b_ref, *rest,
                alpha, emit_next):
    if emit_next:
        xo_ref, uo_ref, acc_ref, h_ref = rest
    else:
        xo_ref, acc_ref, h_ref = rest
    f = pl.program_id(2)
    last = pl.num_programs(2) - 1

    @pl.when(f == 0)
    def _():
        acc_ref[...] = jnp.zeros_like(acc_ref)

    h = jnp.dot(u_ref[0], w1_ref[0], preferred_element_type=F32)
    h_ref[...] = jnp.square(jnp.maximum(h, 0.0)).astype(BF16)

    @pl.when(f < last)
    def _():
        acc_ref[...] += jnp.dot(h_ref[...], w2_ref[0], preferred_element_type=F32)

    @pl.when(f == last)
    def _():
        mod = mod_ref[0, 0]
        for r0 in range(0, x_ref.shape[1], EPILOGUE_ROWS):
            rows = slice(r0, r0 + EPILOGUE_ROWS)
            y = acc_ref[rows, :] + jnp.dot(h_ref[rows, :], w2_ref[0], preferred_element_type=F32)
            xn = _residual_layer_norm(x_ref[0, rows, :], y, mod[5:6], g_ref[0], b_ref[0], alpha)
            xo_ref[0, rows, :] = xn
            if emit_next:
                modn = modn_ref[0, 0]
                uo_ref[0, rows, :] = (xn * (1.0 + modn[1:2]) + modn[0:1]).astype(BF16)


def _ffn(u, w1, w2, x, mod, ln_g, ln_b, layer, alpha, emit_next, tm=512, tf=1024):
    b, s, d = x.shape
    d_ff = w1.shape[2]
    next_layer = layer + 1 if emit_next else layer
    row = lambda i, m, f: (i, m, 0)
    out_specs = [pl.BlockSpec((1, tm, d), row)]
    out_shape = [jax.ShapeDtypeStruct((b, s, d), F32)]
    if emit_next:
        out_specs.append(pl.BlockSpec((1, tm, d), row))
        out_shape.append(jax.ShapeDtypeStruct((b, s, d), BF16))
    return pl.pallas_call(
        functools.partial(_ffn_kernel, alpha=alpha, emit_next=emit_next),
        grid=(b, s // tm, d_ff // tf),
        in_specs=[pl.BlockSpec((1, tm, d), row),
                  pl.BlockSpec((1, d, tf), lambda i, m, f: (layer, 0, f)),
                  pl.BlockSpec((1, tf, d), lambda i, m, f: (layer, f, 0)),
                  pl.BlockSpec((1, tm, d), row),
                  pl.BlockSpec((1, 1, 6, d), lambda i, m, f: (layer, i, 0, 0)),
                  pl.BlockSpec((1, 1, 6, d), lambda i, m, f: (next_layer, i, 0, 0)),
                  pl.BlockSpec((1, 1, d), lambda i, m, f: (layer, 0, 0)),
                  pl.BlockSpec((1, 1, d), lambda i, m, f: (layer, 0, 0))],
        out_specs=out_specs,
        out_shape=out_shape,
        scratch_shapes=[pltpu.VMEM((tm, d), F32), pltpu.VMEM((tm, tf), BF16)],
        compiler_params=_params(("arbitrary", "arbitrary", "arbitrary")),
        name="ffn_ln",
    )(u, w1, w2, x, mod, mod, ln_g, ln_b)


def _rotary_tables(seq):
    inv_freq = ROPE_THETA ** (-jnp.arange(0, HEAD_DIM, 2, dtype=F32) / HEAD_DIM)
    ang = jnp.arange(seq, dtype=F32)[:, None] * inv_freq[None, :]
    cos = jnp.concatenate([jnp.cos(ang), jnp.cos(ang)], axis=-1)
    sin = jnp.concatenate([-jnp.sin(ang), jnp.sin(ang)], axis=-1)
    return cos, sin


def kernel(x, c, w_in, conv_w, mix_norm_g, w_out, w_mod, b_mod, ln1_g, ln1_b, w_ff1, w_ff2,
           ln2_g, ln2_b):
    b, s, d = x.shape
    depth = w_in.shape[0]
    alpha = (2 * depth) ** 0.25
    assert d == D_ATTN + D_CONV and w_in.shape[2] == 3 * D_ATTN + 3 * D_CONV
    assert s % (BAND * DILATIONS[-1]) == 0

    c_pad = jnp.pad(c, ((0, 8 - b), (0, 0)))
    mod = _modulation(c_pad, w_mod, b_mod)[:, :b].reshape(depth, b, 6, d)
    cos, sin = _rotary_tables(s)
    w_ff1_h, w_ff2_h = w_ff1.astype(BF16), w_ff2.astype(BF16)
    mix_gain = mix_norm_g.reshape(depth, 1, d)
    ln1_g, ln1_b, ln2_g, ln2_b = (t.reshape(depth, 1, d) for t in (ln1_g, ln1_b, ln2_g, ln2_b))

    u = _modulate(x, mod, 0)
    for layer in range(depth):
        qk_views = _qk_proj(u, w_in, layer, cos, sin)
        v_views = _v_proj(u, w_in, layer)
        conv = _conv_proj(u, w_in, conv_w, mix_gain, layer)
        attn = _attention(qk_views, v_views, mix_gain, layer)
        x, u = _out_proj(attn, conv, w_out, x, mod, ln1_g, ln1_b, layer, alpha)
        last = layer == depth - 1
        outs = _ffn(u, w_ff1_h, w_ff2_h, x, mod, ln2_g, ln2_b, layer, alpha, emit_next=not last)
        x = outs[0]
        if not last:
            u = outs[1]
    return x
```
